```python
import numpy as np
import jax
import jax.numpy as jnp
from jax import lax

D_MODEL = 1024
BATCH = 8
SEQ = 4096
DEPTH = 4

CTX_LEN = 256
GRID_W = 64
NORM_EPS = 1e-6
CHUNK = 64

HG_HEADS = 4
HG_DK = 64
HG_DV = 64
HG_WIDTH = HG_HEADS * HG_DV
ML_HEADS = 4
ML_DK = 64
ML_DV = 64
ML_WIDTH = ML_HEADS * ML_DV
AT_HEADS = 4
AT_KV_HEADS = 2
AT_HEAD_DIM = 128
AT_WIDTH = AT_HEADS * AT_HEAD_DIM
ROPE_THETA = 10000.0
Q_BLOCK = 128

D_MIX = HG_WIDTH + ML_WIDTH + AT_WIDTH

N_EXPERTS = 32
TOP_K = 4
D_EXPERT = D_MODEL
SWIGLU_LIMIT = 7.0
SWIGLU_ALPHA = 1.702
MOE_BLOCK = 512

HG_SPLITS = (HG_HEADS * HG_DK, HG_HEADS * HG_DK, HG_HEADS * HG_DK, HG_WIDTH, HG_WIDTH)
ML_SPLITS = (ML_HEADS * ML_DK, ML_HEADS * ML_DK, ML_WIDTH, ML_WIDTH, 4 * ML_HEADS)
AT_SPLITS = (AT_WIDTH, AT_KV_HEADS * AT_HEAD_DIM, AT_KV_HEADS * AT_HEAD_DIM)
HG_IN = sum(HG_SPLITS)
ML_IN = sum(ML_SPLITS)
AT_IN = sum(AT_SPLITS)
N_IN = HG_IN + ML_IN + AT_IN

F32 = jnp.float32

kernel_name = 'hybrid_hgrn2_mlstm_gqa_moe_dit'


def split_sizes(a, sizes):
    return jnp.split(a, np.cumsum(sizes)[:-1].tolist(), axis=-1)


def rms_norm(x, g):
    xf = x.astype(F32)
    y = xf * lax.rsqrt(jnp.mean(xf * xf, axis=-1, keepdims=True) + NORM_EPS)
    return (y * g.astype(F32)).astype(x.dtype)


def to_heads(a, n_heads):
    b, t, _ = a.shape
    return a.reshape(b, t, n_heads, -1).transpose(0, 2, 1, 3)


def from_heads(a):
    b, h, t, d = a.shape
    return a.transpose(0, 2, 1, 3).reshape(b, t, h * d)


def to_chunks(a):
    b, h, t = a.shape[:3]
    return jnp.moveaxis(a.reshape(b, h, t // CHUNK, CHUNK, *a.shape[3:]), 2, 0)


def from_chunks(a):
    a = jnp.moveaxis(a, 0, 2)
    b, h, n, c = a.shape[:4]
    return a.reshape(b, h, n * c, *a.shape[4:])


def hgrn2_lower_bounds(lb_logits):
    p = jax.nn.softmax(lb_logits.astype(F32), axis=1)
    cum = jnp.cumsum(p, axis=1)
    return cum - cum[:, :1]


def hgrn2_gates(z, lb):
    z = z.astype(F32)
    log_f = jnp.logaddexp(jnp.log(lb), jnp.log1p(-lb) + jax.nn.log_sigmoid(z))
    k = (1.0 - lb) * jax.nn.sigmoid(-z)
    return log_f, k


def gla_scan(q, k, v, log_f, s0, reverse):
    if reverse:
        q, k, v, log_f = (jnp.flip(a, 2) for a in (q, k, v, log_f))
    causal = jnp.tril(jnp.ones((CHUNK, CHUNK), bool))

    def step(s, blk):
        qc, kc, vc, lf = blk
        b = jnp.cumsum(lf, axis=-2)
        diff = b[..., :, None, :] - b[..., None, :, :]
        decay = jnp.exp(jnp.where(causal[:, :, None], diff, -jnp.inf))
        scores = jnp.einsum('bhtd,bhsd,bhtsd->bhts', qc, kc, decay)
        o = (jnp.einsum('bhts,bhsv->bhtv', scores, vc)
             + jnp.einsum('bhtd,bhdv->bhtv', qc * jnp.exp(b), s))
        b_end = b[..., -1:, :]
        s_new = (jnp.exp(b_end[..., 0, :])[..., None] * s
                 + jnp.einsum('bhsd,bhsv->bhdv', kc * jnp.exp(b_end - b), vc))
        return s_new, o

    s_fin, o = lax.scan(step, s0, tuple(to_chunks(a.astype(F32)) for a in (q, k, v, log_f)))
    o = from_chunks(o)
    if reverse:
        o = jnp.flip(o, 2)
    return o, s_fin


def hgrn2_mixer(p, pc, lb, out_g, with_ctx_out):
    def prep(a):
        q, z_fwd, z_bwd, v, g = split_sizes(a, HG_SPLITS)
        return (to_heads(q, HG_HEADS), (to_heads(z_fwd, HG_HEADS), to_heads(z_bwd, HG_HEADS)),
                to_heads(v, HG_HEADS), g)

    q, z, v, g = prep(p)
    qc, zc, vc, gc = prep(pc)
    s0 = jnp.zeros((p.shape[0], HG_HEADS, HG_DK, HG_DV), F32)
    o = 0.0
    oc = 0.0
    for direction in range(2):
        rev = direction == 1
        lb_d = lb[direction].astype(F32).reshape(HG_HEADS, 1, HG_DK)
        log_f, k = hgrn2_gates(z[direction], lb_d)
        log_fc, kc = hgrn2_gates(zc[direction], lb_d)
        oc_d, s_ctx = gla_scan(qc, kc, vc, log_fc, s0, rev)
        o_d, _ = gla_scan(q, k, v, log_f, s_ctx, rev)
        o = o + o_d
        oc = oc + oc_d

    def finish(o, g):
        return (from_heads(rms_norm(o, out_g)) * jax.nn.silu(g.astype(F32))).astype(g.dtype)

    return finish(o, g), (finish(oc, gc) if with_ctx_out else None)


def mlstm_scan(q, k, v, i_pre, log_f, state, reverse):
    if reverse:
        q, k, v, i_pre, log_f = (jnp.flip(a, 2) for a in (q, k, v, i_pre, log_f))
    causal = jnp.tril(jnp.ones((CHUNK, CHUNK), bool))

    def step(carry, blk):
        c_st, n_st, m_st = carry
        qc, kc, vc, ic, fc = blk
        b = jnp.cumsum(fc, axis=-1)
        d = jnp.where(causal, b[..., :, None] - b[..., None, :] + ic[..., None, :], -jnp.inf)
        from_prev = b + m_st[..., None]
        m_t = jnp.maximum(from_prev, jnp.max(d, axis=-1))
        w = jnp.exp(d - m_t[..., None]) * jnp.einsum('bhtd,bhsd->bhts', qc, kc)
        a_prev = jnp.exp(from_prev - m_t)
        num = (jnp.einsum('bhts,bhsv->bhtv', w, vc)
               + a_prev[..., None] * jnp.einsum('bhtd,bhdv->bhtv', qc, c_st))
        den = jnp.sum(w, axis=-1) + a_prev * jnp.einsum('bhtd,bhd->bht', qc, n_st)
        h = num / jnp.maximum(jnp.abs(den), jnp.exp(-m_t))[..., None]
        m_new = m_t[..., -1]
        a_end = jnp.exp(b[..., -1] + m_st - m_new)
        w_end = jnp.exp(b[..., -1:] - b + ic - m_new[..., None])
        c_new = a_end[..., None, None] * c_st + jnp.einsum('bhs,bhsd,bhsv->bhdv', w_end, kc, vc)
        n_new = a_end[..., None] * n_st + jnp.einsum('bhs,bhsd->bhd', w_end, kc)
        return (c_new, n_new, m_new), h

    state_fin, h = lax.scan(step, state, tuple(to_chunks(a.astype(F32)) for a in (q, k, v, i_pre, log_f)))
    h = from_chunks(h)
    if reverse:
        h = jnp.flip(h, 2)
    return h, state_fin


def mlstm_mixer(p, pc, gate_bias, out_g, with_ctx_out):
    def prep(a):
        q, k, v, og, gates = split_sizes(a, ML_SPLITS)
        gates = gates.astype(F32).reshape(a.shape[0], a.shape[1], 4, ML_HEADS) + gate_bias.astype(F32)
        gates = jnp.moveaxis(gates, 1, -1)
        return (to_heads(q, ML_HEADS), to_heads(k, ML_HEADS) * (ML_DK ** -0.5), to_heads(v, ML_HEADS), og,
                gates[:, 0:2], jax.nn.log_sigmoid(gates[:, 2:4]))

    q, k, v, og, i_pre, log_f = prep(p)
    qc, kc, vc, ogc, i_prec, log_fc = prep(pc)
    b = p.shape[0]
    state0 = (jnp.zeros((b, ML_HEADS, ML_DK, ML_DV), F32), jnp.zeros((b, ML_HEADS, ML_DK), F32),
              jnp.zeros((b, ML_HEADS), F32))
    h = 0.0
    hc = 0.0
    for direction in range(2):
        rev = direction == 1
        hc_d, st_ctx = mlstm_scan(qc, kc, vc, i_prec[:, direction], log_fc[:, direction], state0, rev)
        h_d, _ = mlstm_scan(q, k, v, i_pre[:, direction], log_f[:, direction], st_ctx, rev)
        h = h + h_d
        hc = hc + hc_d

    def finish(h, og):
        return (from_heads(rms_norm(h, out_g)) * jax.nn.sigmoid(og.astype(F32))).astype(og.dtype)

    return finish(h, og), (finish(hc, ogc) if with_ctx_out else None)


def rope_tables(n, dtype):
    rows = n // GRID_W
    row = jnp.broadcast_to(jnp.arange(rows)[:, None], (rows, GRID_W)).reshape(-1).astype(F32)
    col = jnp.broadcast_to(jnp.arange(GRID_W)[None, :], (rows, GRID_W)).reshape(-1).astype(F32)
    axis_dims = AT_HEAD_DIM // 2
    inv_freq = ROPE_THETA ** (-jnp.arange(0, axis_dims, 2, dtype=F32) / axis_dims)
    ang_r = row[:, None] * inv_freq
    ang_c = col[:, None] * inv_freq
    return tuple(t.astype(dtype) for t in (jnp.cos(ang_r), jnp.sin(ang_r), jnp.cos(ang_c), jnp.sin(ang_c)))


def rotate(x, cos, sin):
    x1, x2 = jnp.split(x, 2, axis=-1)
    return jnp.concatenate([x1 * cos - x2 * sin, x2 * cos + x1 * sin], axis=-1)


def rope_2d(x, tables):
    cos_r, sin_r, cos_c, sin_c = tables
    x_row, x_col = jnp.split(x, 2, axis=-1)
    return jnp.concatenate([rotate(x_row, cos_r, sin_r), rotate(x_col, cos_c, sin_c)], axis=-1)


def gqa_attend(qb, keys, vals):
    s = jnp.einsum('bkgqd,bksd->bkgqs', qb, keys).astype(F32) * (AT_HEAD_DIM ** -0.5)
    pr = jax.nn.softmax(s, axis=-1).astype(vals.dtype)
    return jnp.einsum('bkgqs,bksd->bkgqd', pr, vals)


def attn_mixer(p, pc, q_g, k_g, rope, with_ctx_out):
    def prep(a):
        q, k, v = split_sizes(a, AT_SPLITS)
        return (rms_norm(to_heads(q, AT_HEADS), q_g), rms_norm(to_heads(k, AT_KV_HEADS), k_g),
                to_heads(v, AT_KV_HEADS))

    q, k, v = prep(p)
    qc, kc, vc = prep(pc)
    q = rope_2d(q, rope)
    k = rope_2d(k, rope)
    keys = jnp.concatenate([kc, k], axis=2)
    vals = jnp.concatenate([vc, v], axis=2)
    b, _, n, hd = q.shape
    grp = AT_HEADS // AT_KV_HEADS
    qb = jnp.moveaxis(q.reshape(b, AT_KV_HEADS, grp, n // Q_BLOCK, Q_BLOCK, hd), 3, 0)
    ob = lax.map(lambda blk: gqa_attend(blk, keys, vals), qb)
    out = from_heads(jnp.moveaxis(ob, 0, 3).reshape(b, AT_HEADS, n, hd)).astype(p.dtype)
    if not with_ctx_out:
        return out, None
    lc = qc.shape[2]
    oc = gqa_attend(qc.reshape(b, AT_KV_HEADS, grp, lc, hd), kc, vc).reshape(b, AT_HEADS, lc, hd)
    return out, from_heads(oc).astype(pc.dtype)


def moe_ffn(h, w_router, b_router, w_gu, b_gu, w_down, b_down):
    t, d = h.shape
    logits = (h @ w_router + b_router).astype(F32)
    top_v, top_i = lax.top_k(logits, TOP_K)
    gates = jax.nn.softmax(top_v, axis=-1).astype(h.dtype)
    tk = t * TOP_K
    flat_e = top_i.reshape(tk)
    flat_tok = jnp.arange(tk) // TOP_K
    flat_g = gates.reshape(tk)
    order = jnp.argsort(flat_e)
    se = flat_e[order]
    counts = jnp.bincount(flat_e, length=N_EXPERTS)
    padded = (counts + MOE_BLOCK - 1) // MOE_BLOCK * MOE_BLOCK
    pend = jnp.cumsum(padded)
    pstart = pend - padded
    cstart = jnp.cumsum(counts) - counts
    dest = pstart[se] + jnp.arange(tk) - cstart[se]
    nb = -(-tk // MOE_BLOCK) + N_EXPERTS
    tok_buf = jnp.full((nb * MOE_BLOCK,), t, jnp.int32).at[dest].set(flat_tok[order])
    gate_buf = jnp.zeros((nb * MOE_BLOCK,), h.dtype).at[dest].set(flat_g[order])
    block_e = jnp.minimum(jnp.searchsorted(pend, jnp.arange(nb) * MOE_BLOCK, side='right'), N_EXPERTS - 1)
    h_pad = jnp.concatenate([h, jnp.zeros((1, d), h.dtype)], axis=0)

    def body(out, blk):
        tok, g, e = blk
        gu = h_pad[tok] @ w_gu[e] + b_gu[e]
        gate, up = jnp.split(gu, 2, axis=-1)
        gate = jnp.minimum(gate, SWIGLU_LIMIT)
        up = jnp.clip(up, -SWIGLU_LIMIT, SWIGLU_LIMIT)
        act = (up + 1.0) * gate * jax.nn.sigmoid(SWIGLU_ALPHA * gate)
        y = (act @ w_down[e] + b_down[e]) * g[:, None]
        return out.at[tok].add(y), None

    out, _ = lax.scan(body, jnp.zeros((t + 1, d), h.dtype),
                      (tok_buf.reshape(nb, MOE_BLOCK), gate_buf.reshape(nb, MOE_BLOCK), block_e))
    return out[:t]


def modulation(cvec, w, b):
    return jnp.split(jax.nn.silu(cvec) @ w + b, 6, axis=-1)


def modulate(h, shift, scale):
    return h * (1.0 + scale) + shift


def setup_inputs(seed: int = 0) -> dict:
    key = jax.random.key(seed)
    ks = jax.random.split(key, 26)

    def nrm(k, shape, s):
        return jax.random.normal(k, shape, F32) * s

    i_bias = nrm(ks[12], (DEPTH, 2, ML_HEADS), 0.1)
    f_bias = jnp.linspace(3.0, 6.0, ML_HEADS, dtype=F32) + nrm(ks[13], (DEPTH, 2, ML_HEADS), 0.1)
    return {
        'x': nrm(ks[0], (BATCH, SEQ, D_MODEL), 1.0),
        'c': nrm(ks[1], (BATCH, D_MODEL), 1.0),
        'ctx': nrm(ks[2], (BATCH, CTX_LEN, D_MODEL), 1.0),
        'c_ctx': nrm(ks[3], (D_MODEL,), 1.0),
        'w_mod': nrm(ks[4], (DEPTH, D_MODEL, 6 * D_MODEL), 0.5 * D_MODEL ** -0.5),
        'b_mod': nrm(ks[5], (DEPTH, 6 * D_MODEL), 0.01),
        'norm1_g': 1.0 + nrm(ks[6], (DEPTH, D_MODEL), 0.01),
        'norm2_g': 1.0 + nrm(ks[7], (DEPTH, D_MODEL), 0.01),
        'w_in': nrm(ks[8], (DEPTH, D_MODEL, N_IN), D_MODEL ** -0.5),
        'w_out': nrm(ks[9], (DEPTH, D_MIX, D_MODEL), D_MIX ** -0.5),
        'hg_lb_logits': nrm(ks[10], (2, DEPTH, HG_HEADS * HG_DK), 0.5),
        'hg_norm_g': 1.0 + nrm(ks[11], (DEPTH, HG_DV), 0.01),
        'ml_gate_bias': jnp.concatenate([i_bias, f_bias], axis=1),
        'ml_norm_g': 1.0 + nrm(ks[14], (DEPTH, ML_DV), 0.01),
        'q_norm_g': 1.0 + nrm(ks[15], (DEPTH, AT_HEAD_DIM), 0.01),
        'k_norm_g': 1.0 + nrm(ks[16], (DEPTH, AT_HEAD_DIM), 0.01),
        'w_router': nrm(ks[17], (DEPTH, D_MODEL, N_EXPERTS), D_MODEL ** -0.5),
        'b_router': nrm(ks[18], (DEPTH, N_EXPERTS), 0.01),
        'w_gu': nrm(ks[19], (DEPTH, N_EXPERTS, D_MODEL, 2 * D_EXPERT), D_MODEL ** -0.5),
        'b_gu': nrm(ks[20], (DEPTH, N_EXPERTS, 2 * D_EXPERT), 0.01),
        'w_down': nrm(ks[21], (DEPTH, N_EXPERTS, D_EXPERT, D_MODEL), D_EXPERT ** -0.5),
        'b_down': nrm(ks[22], (DEPTH, N_EXPERTS, D_MODEL), 0.01),
    }


def reference(x, c, ctx, c_ctx, w_mod, b_mod, norm1_g, norm2_g, w_in, w_out,
              hg_lb_logits, hg_norm_g, ml_gate_bias, ml_norm_g, q_norm_g, k_norm_g,
              w_router, b_router, w_gu, b_gu, w_down, b_down):
    b, n, d = x.shape
    rope = rope_tables(n, x.dtype)
    hg_lb = hgrn2_lower_bounds(hg_lb_logits)
    xc = ctx
    for layer in range(DEPTH):
        last = layer == DEPTH - 1
        sh1, sc1, g1, sh2, sc2, g2 = [m[:, None, :] for m in modulation(c, w_mod[layer], b_mod[layer])]
        sh1c, sc1c, g1c, sh2c, sc2c, g2c = modulation(c_ctx, w_mod[layer], b_mod[layer])

        h = modulate(rms_norm(x, norm1_g[layer]), sh1, sc1)
        hc = modulate(rms_norm(xc, norm1_g[layer]), sh1c, sc1c)
        p_hg, p_ml, p_at = split_sizes(h @ w_in[layer], (HG_IN, ML_IN, AT_IN))
        pc_hg, pc_ml, pc_at = split_sizes(hc @ w_in[layer], (HG_IN, ML_IN, AT_IN))
        y_hg, yc_hg = hgrn2_mixer(p_hg, pc_hg, hg_lb[:, layer], hg_norm_g[layer], not last)
        y_ml, yc_ml = mlstm_mixer(p_ml, pc_ml, ml_gate_bias[layer], ml_norm_g[layer], not last)
        y_at, yc_at = attn_mixer(p_at, pc_at, q_norm_g[layer], k_norm_g[layer], rope, not last)
        x = x + g1 * (jnp.concatenate([y_hg, y_ml, y_at], axis=-1) @ w_out[layer])

        h2 = modulate(rms_norm(x, norm2_g[layer]), sh2, sc2)
        if last:
            f = moe_ffn(h2.reshape(b * n, d), w_router[layer], b_router[layer], w_gu[layer],
                        b_gu[layer], w_down[layer], b_down[layer])
            x = x + g2 * f.reshape(b, n, d)
        else:
            xc = xc + g1c * (jnp.concatenate([yc_hg, yc_ml, yc_at], axis=-1) @ w_out[layer])
            h2c = modulate(rms_norm(xc, norm2_g[layer]), sh2c, sc2c)
            tokens = jnp.concatenate([h2.reshape(b * n, d), h2c.reshape(-1, d)], axis=0)
            f = moe_ffn(tokens, w_router[layer], b_router[layer], w_gu[layer],
                        b_gu[layer], w_down[layer], b_down[layer])
            x = x + g2 * f[:b * n].reshape(b, n, d)
            xc = xc + g2c * f[b * n:].reshape(xc.shape)
    return x
```

```python
import functools

import numpy as np
import jax
import jax.numpy as jnp
from jax import lax
from jax.experimental import pallas as pl
from jax.experimental.pallas import tpu as pltpu

F32 = jnp.float32
BF16 = jnp.bfloat16
I32 = jnp.int32
HIGHEST = lax.Precision.HIGHEST

D_MODEL = 1024
NORM_EPS = 1e-6
CHUNK = 64
TILE = 256
CHUNKS_PER_TILE = TILE // CHUNK
GRID_W = 64
ROPE_THETA = 10000.0

N_HEADS = 4
HEAD_DIM = 64
MIX_W = N_HEADS * HEAD_DIM
AT_HEADS = 4
AT_KV_HEADS = 2
AT_HEAD_DIM = 128
AT_GROUP = AT_HEADS // AT_KV_HEADS

HG_IN = 5 * MIX_W
ML_IN = 4 * MIX_W
AT_IN = (AT_HEADS + 2 * AT_KV_HEADS) * AT_HEAD_DIM
GATE_LANES = 128

N_EXPERTS = 32
TOP_K = 4
D_EXPERT = 1024
SWIGLU_LIMIT = 7.0
SWIGLU_ALPHA = 1.702
MOE_BLOCK = 512
ROW_SUB = D_MODEL // 128
LANES = 128
N_MOD = 6 * D_MODEL
MOD_ROWS = 16

VMEM_LIMIT = 56 * 1024 * 1024


def _cparams(n_axes, vmem=None):
    kw = dict(dimension_semantics=("arbitrary",) * n_axes)
    if vmem is not None:
        kw["vmem_limit_bytes"] = vmem
    return pltpu.CompilerParams(**kw)


def _dot(a, b, precision=None):
    return jnp.dot(a, b, preferred_element_type=F32, precision=precision)


def _dot_nt(a, b):
    return lax.dot_general(a, b, (((1,), (1,)), ((), ())), preferred_element_type=F32)


def _dot_tn(a, b, precision=None):
    return lax.dot_general(a, b, (((0,), (0,)), ((), ())), preferred_element_type=F32,
                           precision=precision)


def _log_sigmoid(z):
    return jnp.minimum(z, 0.0) - jnp.log1p(jnp.exp(-jnp.abs(z)))


def _rms(x, eps=NORM_EPS):
    return x * lax.rsqrt(jnp.mean(x * x, axis=-1, keepdims=True) + eps)


def _mod_kernel(c_ref, w_ref, b_ref, o_ref):
    cv = c_ref[...]
    s = cv * (1.0 / (1.0 + jnp.exp(-cv)))
    o_ref[0] = _dot(s, w_ref[0], precision=HIGHEST) + b_ref[0]


def _modulation(cvec, w_mod, b_mod):
    depth = w_mod.shape[0]
    nblk = 1536
    return pl.pallas_call(
        _mod_kernel,
        grid=(depth, N_MOD // nblk),
        in_specs=[pl.BlockSpec((MOD_ROWS, D_MODEL), lambda l, n: (0, 0)),
                  pl.BlockSpec((1, D_MODEL, nblk), lambda l, n: (l, 0, n)),
                  pl.BlockSpec((1, 1, nblk), lambda l, n: (l, 0, n))],
        out_specs=pl.BlockSpec((1, MOD_ROWS, nblk), lambda l, n: (l, 0, n)),
        out_shape=jax.ShapeDtypeStruct((depth, MOD_ROWS, N_MOD), F32),
        compiler_params=_cparams(2, VMEM_LIMIT),
        name="modulation",
    )(cvec, w_mod, b_mod.reshape(depth, 1, N_MOD))


def _mod_spec(batch):
    return pl.BlockSpec((1, 1, N_MOD), lambda b, j: (jnp.where(j == 0, batch, b), 0, 0))


def _inproj_kernel(x_ref, mod_ref, g_ref, w_ref, hg_ref, ml_ref, at_ref, gt_ref):
    x = x_ref[0]
    mod = mod_ref[0]
    sh = mod[:, 0:D_MODEL]
    sc = mod[:, D_MODEL:2 * D_MODEL]
    h = _rms(x) * g_ref[...]
    h = (h * (1.0 + sc) + sh).astype(BF16)
    o0 = 0
    for ref, width in ((hg_ref, HG_IN), (ml_ref, ML_IN), (at_ref, AT_IN), (gt_ref, 2 * GATE_LANES)):
        ref[0] = _dot(h, w_ref[:, o0:o0 + width])
        o0 += width


def _inproj(xs, mod_l, g1, w_in_r):
    batch, rows, _ = xs.shape
    nt = rows // TILE
    n_all = HG_IN + ML_IN + AT_IN + 2 * GATE_LANES
    tok = lambda w: pl.BlockSpec((1, TILE, w), lambda b, j: (b, j, 0))
    return pl.pallas_call(
        _inproj_kernel,
        grid=(batch, nt),
        in_specs=[tok(D_MODEL), _mod_spec(batch),
                  pl.BlockSpec((1, D_MODEL), lambda b, j: (0, 0)),
                  pl.BlockSpec((D_MODEL, n_all), lambda b, j: (0, 0))],
        out_specs=[tok(HG_IN), tok(ML_IN), tok(AT_IN), tok(2 * GATE_LANES)],
        out_shape=[jax.ShapeDtypeStruct((batch, rows, w), F32)
                   for w in (HG_IN, ML_IN, AT_IN, 2 * GATE_LANES)],
        compiler_params=_cparams(2, VMEM_LIMIT),
        name="inproj",
    )(xs, mod_l, g1, w_in_r)


def _tile_of(d, j, nt):
    return j + d * jnp.where(j > 0, nt - 2 * j, 0)


def _chunk_of(d, i):
    return i + d * (CHUNKS_PER_TILE - 1 - 2 * i)


_HG_LEVELS = (32, 16, 8, 4, 2, 1)


@functools.lru_cache(maxsize=None)
def _hgrn2_consts():
    c = CHUNK
    tri = np.tril(np.ones((c, c)))
    ops = [tri]
    masks = []
    pos = np.arange(c)
    for m in _HG_LEVELS:
        blk = pos // (2 * m)
        sel = np.zeros((c, c))
        sel[pos, 2 * m * blk + m - 1] = 1.0
        ops.append(sel @ tri)
        in_b = (pos % (2 * m)) >= m
        masks.append(((blk[:, None] == blk[None, :]) & in_b[:, None] & ~in_b[None, :]).astype(np.float64))
    masks.append(np.eye(c))
    ops.append(np.ones((c, c)))
    flip = np.eye(c)[::-1]
    ops_d = np.stack([np.concatenate(ops, 0), np.concatenate([flip @ o @ flip for o in ops], 0)])
    masks_d = np.stack([np.stack(masks), np.stack([mk[::-1, ::-1] for mk in masks])])
    return ops_d.astype(np.float32), masks_d.astype(np.float32)


def _hgrn2_kernel(q_ref, z_ref, v_ref, lb_ref, ops_ref, msk_ref, o_ref, st_scr, lf_scr, k_scr):
    d = pl.program_id(1)
    j = pl.program_id(2)

    @pl.when(j == 0)
    def _():
        st_scr[...] = jnp.zeros_like(st_scr)

    lb = lb_ref[0]
    z = z_ref[0]
    a = jnp.log(lb)
    cterm = jnp.log1p(-lb) + _log_sigmoid(z)
    lf_scr[...] = jnp.maximum(a, cterm) + jnp.log1p(jnp.exp(-jnp.abs(a - cterm)))
    k_scr[...] = (1.0 - lb) / (1.0 + jnp.exp(z))
    n_lv = len(_HG_LEVELS)

    def chunk(i, carry):
        off = pl.multiple_of(_chunk_of(d, i) * CHUNK, CHUNK)
        rows = pl.ds(off, CHUNK)
        cs = _dot(ops_ref[0], lf_scr[rows, :], precision=HIGHEST)
        b = cs[0:CHUNK]
        tot = cs[(n_lv + 1) * CHUNK:(n_lv + 2) * CHUNK]
        q = q_ref[0, rows, :]
        k = k_scr[rows, :]
        vb = v_ref[0, rows, :].astype(BF16)
        q_in = (q * jnp.exp(b)).astype(BF16)
        k_out = (k * jnp.exp(tot - b)).astype(BF16)
        s_decay = jnp.exp(tot[0:1])
        qs, ks = [], []
        for lv in range(n_lv):
            r = cs[(lv + 1) * CHUNK:(lv + 2) * CHUNK]
            qs.append((q * jnp.exp(jnp.minimum(b - r, 0.0))).astype(BF16))
            ks.append((k * jnp.exp(jnp.minimum(r - b, 0.0))).astype(BF16))
        qs.append(q.astype(BF16))
        ks.append(k.astype(BF16))
        for h in range(N_HEADS):
            sl = slice(h * HEAD_DIM, (h + 1) * HEAD_DIM)
            sc = jnp.zeros((CHUNK, CHUNK), F32)
            for lv in range(n_lv + 1):
                sc = sc + msk_ref[0, lv] * _dot_nt(qs[lv][:, sl], ks[lv][:, sl])
            st = st_scr[h]
            o_ref[0, 0, rows, sl] = _dot(sc.astype(BF16), vb[:, sl]) + _dot_nt(q_in[:, sl], st.astype(BF16))
            st_scr[h] = st * s_decay[:, sl] + _dot_tn(vb[:, sl], k_out[:, sl])
        return carry

    lax.fori_loop(0, CHUNKS_PER_TILE, chunk, 0)


def _hgrn2(p_hg, lb_l):
    batch, rows, _ = p_hg.shape
    nt = rows // TILE
    ops_d, masks_d = _hgrn2_consts()
    n_ops = ops_d.shape[1]
    n_msk = masks_d.shape[1]
    col = lambda c: pl.BlockSpec((1, TILE, MIX_W), lambda b, d, j: (b, _tile_of(d, j, nt), c))
    return pl.pallas_call(
        _hgrn2_kernel,
        grid=(batch, 2, nt),
        in_specs=[col(0),
                  pl.BlockSpec((1, TILE, MIX_W), lambda b, d, j: (b, _tile_of(d, j, nt), 1 + d)),
                  col(3),
                  pl.BlockSpec((1, 1, MIX_W), lambda b, d, j: (d, 0, 0)),
                  pl.BlockSpec((1, n_ops, CHUNK), lambda b, d, j: (d, 0, 0)),
                  pl.BlockSpec((1, n_msk, CHUNK, CHUNK), lambda b, d, j: (d, 0, 0, 0))],
        out_specs=pl.BlockSpec((1, 1, TILE, MIX_W), lambda b, d, j: (d, b, _tile_of(d, j, nt), 0)),
        out_shape=jax.ShapeDtypeStruct((2, batch, rows, MIX_W), F32),
        scratch_shapes=[pltpu.VMEM((N_HEADS, HEAD_DIM, HEAD_DIM), F32),
                        pltpu.VMEM((TILE, MIX_W), F32),
                        pltpu.VMEM((TILE, MIX_W), F32)],
        compiler_params=_cparams(3, VMEM_LIMIT),
        name="hgrn2",
    )(p_hg, p_hg, p_hg, lb_l.reshape(2, 1, MIX_W), jnp.asarray(ops_d), jnp.asarray(masks_d))


@functools.lru_cache(maxsize=None)
def _mlstm_consts():
    c = CHUNK
    tri = np.tril(np.ones((c, c)))
    flip = np.eye(c)[::-1]
    col_ops, row_ops, causal = [], [], []
    for t in (tri, flip @ tri @ flip):
        col_ops.append(np.concatenate([t, np.ones((c, c))], 0))
        row_ops.append(np.concatenate([t.T, np.eye(c)], 1))
        causal.append(t)
    f = lambda xs: np.stack(xs).astype(np.float32)
    return f(col_ops), f(row_ops), f(causal)


def _mlstm_kernel(q_ref, k_ref, v_ref, g_ref, bias_ref, cop_ref, rop_ref, cm_ref, h_ref,
                  c_scr, n_scr, m_scr, g_scr):
    d = pl.program_id(1)
    j = pl.program_id(2)

    @pl.when(j == 0)
    def _():
        c_scr[...] = jnp.zeros_like(c_scr)
        n_scr[...] = jnp.zeros_like(n_scr)
        m_scr[...] = jnp.zeros_like(m_scr)

    g = g_ref[0] + bias_ref[0]
    lane = lax.broadcasted_iota(I32, g.shape, 1)
    g_scr[...] = jnp.where(lane >= N_HEADS, _log_sigmoid(g), g)
    neg_inf = F32(-jnp.inf)

    def chunk(i, carry):
        off = pl.multiple_of(_chunk_of(d, i) * CHUNK, CHUNK)
        rows = pl.ds(off, CHUNK)
        gc = g_scr[rows, :]
        cols = _dot(cop_ref[0], gc, precision=HIGHEST)
        rws = _dot_tn(gc, rop_ref[0], precision=HIGHEST)
        causal = cm_ref[0] > 0.5
        q = q_ref[0, rows, :]
        k = k_ref[0, rows, :] * (HEAD_DIM ** -0.5)
        vb = v_ref[0, rows, :].astype(BF16)
        qb = q.astype(BF16)
        kb = k.astype(BF16)
        for h in range(N_HEADS):
            sl = slice(h * HEAD_DIM, (h + 1) * HEAD_DIM)
            fl = N_HEADS + h
            b_col = cols[0:CHUNK, fl:fl + 1]
            tot = cols[CHUNK:CHUNK + 1, fl:fl + 1]
            i_col = gc[:, h:h + 1]
            b_row = rws[fl:fl + 1, 0:CHUNK]
            i_row = rws[h:h + 1, CHUNK:2 * CHUNK]
            m_st = m_scr[h][:, 0:1]
            dm = jnp.where(causal, b_col - b_row + i_row, neg_inf)
            from_prev = b_col + m_st
            m_t = jnp.maximum(from_prev, jnp.max(dm, axis=-1, keepdims=True))
            w = jnp.exp(dm - m_t) * _dot_nt(qb[:, sl], kb[:, sl])
            a_prev = jnp.exp(from_prev - m_t)
            c_st = c_scr[h]
            n_st = n_scr[h]
            num = _dot(w.astype(BF16), vb[:, sl]) + a_prev * _dot(qb[:, sl], c_st.astype(BF16))
            den = jnp.sum(w, axis=-1, keepdims=True) + a_prev * jnp.sum(q[:, sl] * n_st, axis=-1, keepdims=True)
            h_ref[0, 0, rows, sl] = num / jnp.maximum(jnp.abs(den), jnp.exp(-m_t))
            e_col = tot - b_col + i_col
            m_new = jnp.maximum(tot + m_st, jnp.max(e_col, axis=0, keepdims=True))
            a_end = jnp.exp(tot + m_st - m_new)
            kw = k[:, sl] * jnp.exp(e_col - m_new)
            c_scr[h] = a_end * c_st + _dot_tn(kw.astype(BF16), vb[:, sl])
            n_scr[h] = a_end * n_st + jnp.sum(kw, axis=0, keepdims=True)
            m_scr[h] = jnp.broadcast_to(m_new, (1, LANES))
        return carry

    lax.fori_loop(0, CHUNKS_PER_TILE, chunk, 0)


def _mlstm(p_ml, p_gt, bias_d):
    batch, rows, _ = p_ml.shape
    nt = rows // TILE
    col_ops, row_ops, causal = _mlstm_consts()
    col = lambda c: pl.BlockSpec((1, TILE, MIX_W), lambda b, d, j: (b, _tile_of(d, j, nt), c))
    by_dir = lambda shp: pl.BlockSpec((1,) + shp, lambda b, d, j: (d,) + (0,) * len(shp))
    return pl.pallas_call(
        _mlstm_kernel,
        grid=(batch, 2, nt),
        in_specs=[col(0), col(1), col(2),
                  pl.BlockSpec((1, TILE, GATE_LANES), lambda b, d, j: (b, _tile_of(d, j, nt), d)),
                  by_dir((1, GATE_LANES)), by_dir((2 * CHUNK, CHUNK)), by_dir((CHUNK, 2 * CHUNK)),
                  by_dir((CHUNK, CHUNK))],
        out_specs=pl.BlockSpec((1, 1, TILE, MIX_W), lambda b, d, j: (d, b, _tile_of(d, j, nt), 0)),
        out_shape=jax.ShapeDtypeStruct((2, batch, rows, MIX_W), F32),
        scratch_shapes=[pltpu.VMEM((N_HEADS, HEAD_DIM, HEAD_DIM), F32),
                        pltpu.VMEM((N_HEADS, 1, HEAD_DIM), F32),
                        pltpu.VMEM((N_HEADS, 1, LANES), F32),
                        pltpu.VMEM((TILE, GATE_LANES), F32)],
        compiler_params=_cparams(3, VMEM_LIMIT),
        name="mlstm",
    )(p_ml, p_ml, p_ml, p_gt, bias_d, jnp.asarray(col_ops), jnp.asarray(row_ops), jnp.asarray(causal))


def _rope_tables(rows, ctx_len):
    n = rows - ctx_len
    r = jnp.arange(n) // GRID_W
    cc = jnp.arange(n) % GRID_W
    axis_dims = AT_HEAD_DIM // 2
    inv_freq = ROPE_THETA ** (-jnp.arange(0, axis_dims, 2, dtype=F32) / axis_dims)
    ang_r = r.astype(F32)[:, None] * inv_freq
    ang_c = cc.astype(F32)[:, None] * inv_freq
    cos = jnp.concatenate([jnp.cos(ang_r), jnp.cos(ang_r), jnp.cos(ang_c), jnp.cos(ang_c)], -1)
    sin = jnp.concatenate([-jnp.sin(ang_r), jnp.sin(ang_r), -jnp.sin(ang_c), jnp.sin(ang_c)], -1)
    cos = jnp.concatenate([jnp.ones((ctx_len, AT_HEAD_DIM), F32), cos], 0)
    sin = jnp.concatenate([jnp.zeros((ctx_len, AT_HEAD_DIM), F32), sin], 0)
    return cos, sin


def _attn_prep_kernel(p_ref, cos_ref, sin_ref, qg_ref, kg_ref, q_ref, k_ref, v_ref):
    cos = cos_ref[...]
    sin = sin_ref[...]
    lane = lax.broadcasted_iota(I32, (TILE, AT_HEAD_DIM), 1)
    first_half = (lane % (AT_HEAD_DIM // 2)) < (AT_HEAD_DIM // 4)
    quarter = AT_HEAD_DIM // 4

    def norm_rope(x, g):
        y = _rms(x) * g
        partner = jnp.where(first_half, pltpu.roll(y, AT_HEAD_DIM - quarter, axis=1), pltpu.roll(y, quarter, axis=1))
        return y * cos + partner * sin

    scale = AT_HEAD_DIM ** -0.5
    for h in range(AT_HEADS):
        sl = slice(h * AT_HEAD_DIM, (h + 1) * AT_HEAD_DIM)
        q_ref[0, :, sl] = (norm_rope(p_ref[0, :, sl], qg_ref[...]) * scale).astype(BF16)
    for h in range(AT_KV_HEADS):
        sl = slice(h * AT_HEAD_DIM, (h + 1) * AT_HEAD_DIM)
        src = slice((AT_HEADS + h) * AT_HEAD_DIM, (AT_HEADS + h + 1) * AT_HEAD_DIM)
        k_ref[0, :, sl] = norm_rope(p_ref[0, :, src], kg_ref[...]).astype(BF16)
    v0 = (AT_HEADS + AT_KV_HEADS) * AT_HEAD_DIM
    v_ref[0] = p_ref[0, :, v0:v0 + AT_KV_HEADS * AT_HEAD_DIM].astype(BF16)


def _attn_prep(p_at, cos, sin, q_g, k_g):
    batch, rows, _ = p_at.shape
    nt = rows // TILE
    tok = lambda w: pl.BlockSpec((1, TILE, w), lambda b, j: (b, j, 0))
    tab = pl.BlockSpec((TILE, AT_HEAD_DIM), lambda b, j: (j, 0))
    vec = pl.BlockSpec((1, AT_HEAD_DIM), lambda b, j: (0, 0))
    kvw = AT_KV_HEADS * AT_HEAD_DIM
    return pl.pallas_call(
        _attn_prep_kernel,
        grid=(batch, nt),
        in_specs=[tok(AT_IN), tab, tab, vec, vec],
        out_specs=[tok(AT_HEADS * AT_HEAD_DIM), tok(kvw), tok(kvw)],
        out_shape=[jax.ShapeDtypeStruct((batch, rows, AT_HEADS * AT_HEAD_DIM), BF16),
                   jax.ShapeDtypeStruct((batch, rows, kvw), BF16),
                   jax.ShapeDtypeStruct((batch, rows, kvw), BF16)],
        compiler_params=_cparams(2, VMEM_LIMIT),
        name="attn_prep",
    )(p_at, cos, sin, q_g, k_g)


def _attn_kernel(q_ref, k_ref, v_ref, o_ref):
    j = pl.program_id(2)

    def attend(n_keys):
        kk = k_ref[0, 0:n_keys, :]
        vv = v_ref[0, 0:n_keys, :]
        for g in range(AT_GROUP):
            sl = slice(g * AT_HEAD_DIM, (g + 1) * AT_HEAD_DIM)
            s = _dot_nt(q_ref[0, :, sl], kk)
            p = jnp.exp(s - jnp.max(s, axis=-1, keepdims=True))
            l = jnp.sum(p, axis=-1, keepdims=True)
            o_ref[0, :, sl] = _dot(p.astype(BF16), vv) / l

    @pl.when(j == 0)
    def _():
        attend(TILE)

    @pl.when(j > 0)
    def _():
        attend(k_ref.shape[1])


def _attention(qn, kn, vb):
    batch, rows, _ = qn.shape
    nt = rows // TILE
    gw = AT_GROUP * AT_HEAD_DIM
    return pl.pallas_call(
        _attn_kernel,
        grid=(batch, AT_KV_HEADS, nt),
        in_specs=[pl.BlockSpec((1, TILE, gw), lambda b, h, j: (b, j, h)),
                  pl.BlockSpec((1, rows, AT_HEAD_DIM), lambda b, h, j: (b, 0, h)),
                  pl.BlockSpec((1, rows, AT_HEAD_DIM), lambda b, h, j: (b, 0, h))],
        out_specs=pl.BlockSpec((1, TILE, gw), lambda b, h, j: (b, j, h)),
        out_shape=jax.ShapeDtypeStruct((batch, rows, AT_HEADS * AT_HEAD_DIM), F32),
        compiler_params=_cparams(3, VMEM_LIMIT),
        name="attention",
    )(qn, kn, vb)


def _outproj_kernel(x_ref, mod_ref, ohf_ref, ohb_ref, gh_ref, omf_ref, omb_ref, gm_ref, at_ref,
                    hgg_ref, mlg_ref, ones_ref, wo_ref, g2_ref, wr_ref, br_ref, low_ref,
                    xo_ref, h2_ref, idx_ref, gate_ref, cnt_ref, cnt_scr):
    b = pl.program_id(0)
    j = pl.program_id(1)

    @pl.when((b == 0) & (j == 0))
    def _():
        cnt_scr[...] = jnp.zeros_like(cnt_scr)

    mod = mod_ref[0]
    gate1 = mod[:, 2 * D_MODEL:3 * D_MODEL]
    sh2 = mod[:, 3 * D_MODEL:4 * D_MODEL]
    sc2 = mod[:, 4 * D_MODEL:5 * D_MODEL]

    def head_norm(o, g):
        ms = _dot(o * o, ones_ref[...], precision=HIGHEST) * (1.0 / HEAD_DIM)
        return o * lax.rsqrt(ms + NORM_EPS) * g

    gh = gh_ref[0]
    y_hg = head_norm(ohf_ref[0, 0] + ohb_ref[0, 0], hgg_ref[...]) * (gh / (1.0 + jnp.exp(-gh)))
    y_ml = head_norm(omf_ref[0, 0] + omb_ref[0, 0], mlg_ref[...]) / (1.0 + jnp.exp(-gm_ref[0]))
    proj = (_dot(y_hg.astype(BF16), wo_ref[0:MIX_W, :])
            + _dot(y_ml.astype(BF16), wo_ref[MIX_W:2 * MIX_W, :])
            + _dot(at_ref[0].astype(BF16), wo_ref[2 * MIX_W:, :]))
    x = x_ref[0] + gate1 * proj
    xo_ref[0] = x
    h2 = _rms(x) * g2_ref[...]
    h2 = h2 * (1.0 + sc2) + sh2
    h2_ref[...] = h2.reshape(TILE, ROW_SUB, LANES).astype(BF16)

    lane = lax.broadcasted_iota(I32, (TILE, LANES), 1)
    logits = _dot(h2, wr_ref[...], precision=HIGHEST) + br_ref[...]
    lg = jnp.where(lane < N_EXPERTS, logits, -jnp.inf)
    multi = jnp.zeros((TILE, LANES), F32)
    hots, vals, ids = [], [], []
    for _ in range(TOP_K):
        mx = jnp.max(lg, axis=-1, keepdims=True)
        idx = jnp.min(jnp.where(lg == mx, lane, LANES), axis=-1, keepdims=True)
        hot = lane == idx
        lg = jnp.where(hot, -jnp.inf, lg)
        multi = multi + hot.astype(F32)
        hots.append(hot)
        vals.append(mx)
        ids.append(idx)
    es = [jnp.exp(v - vals[0]) for v in vals]
    den = es[0] + es[1] + es[2] + es[3]
    before = _dot(low_ref[...], multi.astype(BF16)) + cnt_scr[...]
    idx_out = jnp.zeros((TILE, LANES), I32)
    gate_out = jnp.zeros((TILE, LANES), F32)
    for kk in range(TOP_K):
        rank = jnp.sum(jnp.where(hots[kk], before, 0.0), axis=-1, keepdims=True).astype(I32)
        idx_out = idx_out + jnp.where(lane == kk, ids[kk], 0) + jnp.where(lane == TOP_K + kk, rank, 0)
        gate_out = gate_out + jnp.where(lane == kk, es[kk] / den, 0.0)
    idx_ref[0] = idx_out
    gate_ref[0] = gate_out
    cnt_scr[...] = cnt_scr[...] + jnp.sum(multi, axis=0, keepdims=True)
    cnt_ref[...] = cnt_scr[...]


def _outproj(xs, mod_l, o_hg, p_hg, o_ml, p_ml, att, hg_g, ml_g, w_out_b, g2, w_router_p, b_router_p):
    batch, rows, _ = xs.shape
    nt = rows // TILE
    tok = lambda w, c=0: pl.BlockSpec((1, TILE, w), lambda b, j: (b, j, c))
    dirs = lambda d: pl.BlockSpec((1, 1, TILE, MIX_W), lambda b, j: (d, b, j, 0))
    full = lambda r, c: pl.BlockSpec((r, c), lambda b, j: (0, 0))
    block_ones = np.kron(np.eye(N_HEADS), np.ones((HEAD_DIM, HEAD_DIM))).astype(np.float32)
    strict_low = np.tril(np.ones((TILE, TILE)), -1).astype(np.float32)
    return pl.pallas_call(
        _outproj_kernel,
        grid=(batch, nt),
        in_specs=[tok(D_MODEL), _mod_spec(batch),
                  dirs(0), dirs(1), tok(MIX_W, 4),
                  dirs(0), dirs(1), tok(MIX_W, 3),
                  tok(AT_HEADS * AT_HEAD_DIM),
                  full(1, MIX_W), full(1, MIX_W), full(MIX_W, MIX_W),
                  full(D_MODEL, D_MODEL), full(1, D_MODEL),
                  full(D_MODEL, LANES), full(1, LANES), full(TILE, TILE)],
        out_specs=[tok(D_MODEL),
                   pl.BlockSpec((TILE, ROW_SUB, LANES), lambda b, j: (b * nt + j, 0, 0)),
                   tok(LANES), tok(LANES),
                   pl.BlockSpec((1, LANES), lambda b, j: (0, 0))],
        out_shape=[jax.ShapeDtypeStruct((batch, rows, D_MODEL), F32),
                   jax.ShapeDtypeStruct((batch * rows, ROW_SUB, LANES), BF16),
                   jax.ShapeDtypeStruct((batch, rows, LANES), I32),
                   jax.ShapeDtypeStruct((batch, rows, LANES), F32),
                   jax.ShapeDtypeStruct((1, LANES), F32)],
        scratch_shapes=[pltpu.VMEM((1, LANES), F32)],
        compiler_params=_cparams(2, VMEM_LIMIT),
        name="outproj_router",
    )(xs, mod_l, o_hg, o_hg, p_hg, o_ml, o_ml, p_ml, att, hg_g, ml_g, jnp.asarray(block_ones),
      w_out_b, g2, w_router_p, b_router_p, jnp.asarray(strict_low, dtype=BF16))


def _dispatch_kernel(dest_ref, h_ref, zero_ref, o_ref, sem):
    del zero_ref

    def row_copy(r, slot):
        return pltpu.make_async_copy(h_ref.at[r], o_ref.at[slot], sem)

    def issue(r, carry):
        for kk in range(TOP_K):
            row_copy(r, dest_ref[0, 0, r * TOP_K + kk]).start()
        return carry

    lax.fori_loop(0, TILE, issue, 0)

    def drain(r, carry):
        for kk in range(TOP_K):
            row_copy(r, dest_ref[0, 0, r * TOP_K + kk]).wait()
        return carry

    lax.fori_loop(0, TILE, drain, 0)


def _dispatch(dest_t, h2, n_slots):
    n_tiles = dest_t.shape[0]
    zeros = jnp.zeros((n_slots, ROW_SUB, LANES), BF16)
    return pl.pallas_call(
        _dispatch_kernel,
        grid=(n_tiles,),
        in_specs=[pl.BlockSpec((1, 1, TILE * TOP_K), lambda t: (t, 0, 0), memory_space=pltpu.SMEM),
                  pl.BlockSpec((TILE, ROW_SUB, LANES), lambda t: (t, 0, 0)),
                  pl.BlockSpec(memory_space=pl.ANY)],
        out_specs=pl.BlockSpec(memory_space=pl.ANY),
        out_shape=jax.ShapeDtypeStruct((n_slots, ROW_SUB, LANES), BF16),
        scratch_shapes=[pltpu.SemaphoreType.DMA(())],
        input_output_aliases={2: 0},
        compiler_params=_cparams(1, VMEM_LIMIT),
        name="moe_dispatch",
    )(dest_t, h2, zeros)


def _expert_kernel(be_ref, nv_ref, x_ref, wgu_ref, bgu_ref, wd_ref, bd_ref, y_ref, wgu_scr, wd_scr):
    i = pl.program_id(0)

    @pl.when(i < nv_ref[0])
    def _():
        prev = be_ref[jnp.maximum(i - 1, 0)]

        @pl.when((i == 0) | (prev != be_ref[i]))
        def _():
            wgu_scr[...] = wgu_ref[0].astype(BF16)
            wd_scr[...] = wd_ref[0].astype(BF16)

        x = x_ref[...].reshape(MOE_BLOCK, D_MODEL)
        gu = _dot(x, wgu_scr[...]) + bgu_ref[0]
        gate = jnp.minimum(gu[:, 0:D_EXPERT], SWIGLU_LIMIT)
        up = jnp.clip(gu[:, D_EXPERT:], -SWIGLU_LIMIT, SWIGLU_LIMIT)
        act = (up + 1.0) * gate * (1.0 / (1.0 + jnp.exp(-SWIGLU_ALPHA * gate)))
        y = _dot(act.astype(BF16), wd_scr[...]) + bd_ref[0]
        y_ref[...] = y.reshape(MOE_BLOCK, ROW_SUB, LANES)

    @pl.when(i >= nv_ref[0])
    def _():
        y_ref[...] = jnp.zeros_like(y_ref)


def _experts(block_e, n_valid, xs_sorted, w_gu, b_gu, w_down, b_down):
    n_slots = xs_sorted.shape[0]
    nb = n_slots // MOE_BLOCK
    clamp = lambda i, nv: jnp.minimum(i, jnp.maximum(nv[0] - 1, 0))
    grid_spec = pltpu.PrefetchScalarGridSpec(
        num_scalar_prefetch=2,
        grid=(nb,),
        in_specs=[pl.BlockSpec((MOE_BLOCK, ROW_SUB, LANES), lambda i, be, nv: (clamp(i, nv), 0, 0)),
                  pl.BlockSpec((1, D_MODEL, 2 * D_EXPERT), lambda i, be, nv: (be[i], 0, 0)),
                  pl.BlockSpec((1, 1, 2 * D_EXPERT), lambda i, be, nv: (be[i], 0, 0)),
                  pl.BlockSpec((1, D_EXPERT, D_MODEL), lambda i, be, nv: (be[i], 0, 0)),
                  pl.BlockSpec((1, 1, D_MODEL), lambda i, be, nv: (be[i], 0, 0))],
        out_specs=pl.BlockSpec((MOE_BLOCK, ROW_SUB, LANES), lambda i, be, nv: (i, 0, 0)),
        scratch_shapes=[pltpu.VMEM((D_MODEL, 2 * D_EXPERT), BF16),
                        pltpu.VMEM((D_EXPERT, D_MODEL), BF16)],
    )
    return pl.pallas_call(
        _expert_kernel,
        grid_spec=grid_spec,
        out_shape=jax.ShapeDtypeStruct((n_slots, ROW_SUB, LANES), F32),
        compiler_params=_cparams(1, VMEM_LIMIT),
        name="moe_experts",
    )(block_e, n_valid, xs_sorted, w_gu, b_gu.reshape(N_EXPERTS, 1, 2 * D_EXPERT),
      w_down, b_down.reshape(N_EXPERTS, 1, D_MODEL))


def _combine_kernel(dest_ref, x_ref, mod_ref, gate_ref, y_ref, o_ref, buf, sem):
    def row_copy(r, kk):
        return pltpu.make_async_copy(y_ref.at[dest_ref[0, 0, r * TOP_K + kk]], buf.at[kk, r], sem)

    def issue(r, carry):
        for kk in range(TOP_K):
            row_copy(r, kk).start()
        return carry

    lax.fori_loop(0, TILE, issue, 0)

    def drain(r, carry):
        for kk in range(TOP_K):
            row_copy(r, kk).wait()
        return carry

    lax.fori_loop(0, TILE, drain, 0)

    gates = gate_ref[0]
    f = jnp.zeros((TILE, D_MODEL), F32)
    for kk in range(TOP_K):
        f = f + buf[kk].reshape(TILE, D_MODEL) * gates[:, kk:kk + 1]
    gate2 = mod_ref[0][:, 5 * D_MODEL:6 * D_MODEL]
    o_ref[0] = x_ref[0] + gate2 * f


def _combine(dest_t, xs, mod_l, gates, y_sorted):
    batch, rows, _ = xs.shape
    nt = rows // TILE
    return pl.pallas_call(
        _combine_kernel,
        grid=(batch, nt),
        in_specs=[pl.BlockSpec((1, 1, TILE * TOP_K), lambda b, j: (b * nt + j, 0, 0), memory_space=pltpu.SMEM),
                  pl.BlockSpec((1, TILE, D_MODEL), lambda b, j: (b, j, 0)),
                  _mod_spec(batch),
                  pl.BlockSpec((1, TILE, LANES), lambda b, j: (b, j, 0)),
                  pl.BlockSpec(memory_space=pl.ANY)],
        out_specs=pl.BlockSpec((1, TILE, D_MODEL), lambda b, j: (b, j, 0)),
        out_shape=jax.ShapeDtypeStruct((batch, rows, D_MODEL), F32),
        scratch_shapes=[pltpu.VMEM((TOP_K, TILE, ROW_SUB, LANES), F32),
                        pltpu.SemaphoreType.DMA(())],
        compiler_params=_cparams(2, VMEM_LIMIT),
        name="moe_combine",
    )(dest_t, xs, mod_l, gates, y_sorted)


def _moe_plan(idx, counts, n_tokens):
    cnt = counts[0, :N_EXPERTS].astype(I32)
    padded = (cnt + MOE_BLOCK - 1) // MOE_BLOCK * MOE_BLOCK
    pend = jnp.cumsum(padded)
    pstart = pend - padded
    top_i = idx[..., 0:TOP_K]
    rank = idx[..., TOP_K:2 * TOP_K]
    dest = (pstart[top_i] + rank).reshape(n_tokens // TILE, 1, TILE * TOP_K)
    nb = -(-(n_tokens * TOP_K) // MOE_BLOCK) + N_EXPERTS
    block_e = jnp.minimum(jnp.searchsorted(pend, jnp.arange(nb, dtype=I32) * MOE_BLOCK, side="right"),
                          N_EXPERTS - 1).astype(I32)
    n_valid = (pend[-1] // MOE_BLOCK).astype(I32).reshape(1)
    return dest, block_e, n_valid, nb * MOE_BLOCK


def _arrange_w_in(w_in_l):
    hg = w_in_l[:, 0:HG_IN]
    ml0 = HG_IN
    ml = w_in_l[:, ml0:ml0 + ML_IN]
    gt = w_in_l[:, ml0 + ML_IN:ml0 + ML_IN + 4 * N_HEADS]
    at = w_in_l[:, ml0 + ML_IN + 4 * N_HEADS:]
    pad = jnp.zeros((D_MODEL, GATE_LANES - 2 * N_HEADS), w_in_l.dtype)
    sel = lambda a, b: jnp.concatenate([gt[:, a * N_HEADS:(a + 1) * N_HEADS], gt[:, b * N_HEADS:(b + 1) * N_HEADS], pad], 1)
    return jnp.concatenate([hg, ml, at, sel(0, 2), sel(1, 3)], 1).astype(BF16)


def _hgrn2_lower_bounds(lb_logits):
    p = jax.nn.softmax(lb_logits.astype(F32), axis=1)
    cum = jnp.cumsum(p, axis=1)
    return cum - cum[:, :1]


def kernel(x, c, ctx, c_ctx, w_mod, b_mod, norm1_g, norm2_g, w_in, w_out, hg_lb_logits, hg_norm_g,
           ml_gate_bias, ml_norm_g, q_norm_g, k_norm_g, w_router, b_router, w_gu, b_gu, w_down, b_down):
    batch, seq, d = x.shape
    ctx_len = ctx.shape[1]
    depth = w_mod.shape[0]
    assert d == D_MODEL and ctx_len == TILE and seq % TILE == 0 and batch + 1 <= MOD_ROWS
    rows = ctx_len + seq
    n_tokens = batch * rows

    cvec = jnp.concatenate([c, c_ctx[None, :], jnp.zeros((MOD_ROWS - batch - 1, d), F32)], 0)
    mod = _modulation(cvec, w_mod, b_mod).reshape(depth, MOD_ROWS, 1, N_MOD)
    hg_lb = _hgrn2_lower_bounds(hg_lb_logits)
    cos, sin = _rope_tables(rows, ctx_len)
    xs = jnp.concatenate([ctx, x], axis=1)

    for layer in range(depth):
        mod_l = mod[layer]
        p_hg, p_ml, p_at, p_gt = _inproj(xs, mod_l, norm1_g[layer][None, :], _arrange_w_in(w_in[layer]))
        o_hg = _hgrn2(p_hg, hg_lb[:, layer])
        gb = ml_gate_bias[layer]
        bias_d = jnp.zeros((2, 1, GATE_LANES), F32)
        bias_d = bias_d.at[0, 0, 0:2 * N_HEADS].set(jnp.concatenate([gb[0], gb[2]]))
        bias_d = bias_d.at[1, 0, 0:2 * N_HEADS].set(jnp.concatenate([gb[1], gb[3]]))
        o_ml = _mlstm(p_ml, p_gt, bias_d)
        qn, kn, vb = _attn_prep(p_at, cos, sin, q_norm_g[layer][None, :], k_norm_g[layer][None, :])
        att = _attention(qn, kn, vb)
        w_router_p = jnp.pad(w_router[layer], ((0, 0), (0, LANES - N_EXPERTS)))
        b_router_p = jnp.pad(b_router[layer], (0, LANES - N_EXPERTS))[None, :]
        xs, h2, idx, gates, counts = _outproj(
            xs, mod_l, o_hg, p_hg, o_ml, p_ml, att,
            jnp.tile(hg_norm_g[layer], N_HEADS)[None, :], jnp.tile(ml_norm_g[layer], N_HEADS)[None, :],
            w_out[layer].astype(BF16), norm2_g[layer][None, :], w_router_p, b_router_p)
        dest, block_e, n_valid, n_slots = _moe_plan(idx, counts, n_tokens)
        xs_sorted = _dispatch(dest, h2, n_slots)
        y_sorted = _experts(block_e, n_valid, xs_sorted, w_gu[layer], b_gu[layer], w_down[layer], b_down[layer])
        xs = _combine(dest, xs, mod_l, gates, y_sorted)
    return xs[:, ctx_len:, :]
```

```python
import functools

import numpy as np
import jax
import jax.numpy as jnp
from jax import lax
from jax.experimental import pallas as pl
from jax.experimental.pallas import tpu as pltpu

F32 = jnp.float32
BF16 = jnp.bfloat16
I32 = jnp.int32
HIGHEST = lax.Precision.HIGHEST

D_MODEL = 1024
NORM_EPS = 1e-6
CHUNK = 64
TILE = 256
CHUNKS_PER_TILE = TILE // CHUNK
GRID_W = 64
ROPE_THETA = 10000.0

N_HEADS = 4
HEAD_DIM = 64
MIX_W = N_HEADS * HEAD_DIM
AT_HEADS = 4
AT_KV_HEADS = 2
AT_HEAD_DIM = 128
AT_GROUP = AT_HEADS // AT_KV_HEADS
ATT_KEY_CHUNK = 1024

HG_IN = 5 * MIX_W
ML_IN = 4 * MIX_W
AT_IN = (AT_HEADS + 2 * AT_KV_HEADS) * AT_HEAD_DIM
GATE_W = 4 * MIX_W

N_EXPERTS = 32
TOP_K = 4
D_EXPERT = 1024
SWIGLU_LIMIT = 7.0
SWIGLU_ALPHA = 1.702
MOE_BLOCK = 512
ROW_SUB = D_MODEL // 128
LANES = 128
N_MOD = 6 * D_MODEL
MOD_ROWS = 16

VMEM_LIMIT = 56 * 1024 * 1024


def _cparams(n_axes, vmem=None):
    kw = dict(dimension_semantics=("arbitrary",) * n_axes)
    if vmem is not None:
        kw["vmem_limit_bytes"] = vmem
    return pltpu.CompilerParams(**kw)


def _dot(a, b, precision=None):
    return jnp.dot(a, b, preferred_element_type=F32, precision=precision)


def _dot_nt(a, b):
    return lax.dot_general(a, b, (((1,), (1,)), ((), ())), preferred_element_type=F32)


def _dot_tn(a, b, precision=None):
    return lax.dot_general(a, b, (((0,), (0,)), ((), ())), preferred_element_type=F32,
                           precision=precision)


def _log_sigmoid(z):
    return jnp.minimum(z, 0.0) - jnp.log1p(jnp.exp(-jnp.abs(z)))


def _rms(x, eps=NORM_EPS):
    return x * lax.rsqrt(jnp.mean(x * x, axis=-1, keepdims=True) + eps)


def _mod_kernel(c_ref, w_ref, b_ref, o_ref):
    cv = c_ref[...]
    s = cv * (1.0 / (1.0 + jnp.exp(-cv)))
    o_ref[0] = _dot(s, w_ref[0], precision=HIGHEST) + b_ref[0]


def _modulation(cvec, w_mod, b_mod):
    depth = w_mod.shape[0]
    nblk = 1536
    return pl.pallas_call(
        _mod_kernel,
        grid=(depth, N_MOD // nblk),
        in_specs=[pl.BlockSpec((MOD_ROWS, D_MODEL), lambda l, n: (0, 0)),
                  pl.BlockSpec((1, D_MODEL, nblk), lambda l, n: (l, 0, n)),
                  pl.BlockSpec((1, 1, nblk), lambda l, n: (l, 0, n))],
        out_specs=pl.BlockSpec((1, MOD_ROWS, nblk), lambda l, n: (l, 0, n)),
        out_shape=jax.ShapeDtypeStruct((depth, MOD_ROWS, N_MOD), F32),
        compiler_params=_cparams(2, VMEM_LIMIT),
        name="modulation",
    )(cvec, w_mod, b_mod.reshape(depth, 1, N_MOD))


def _mod_spec(batch):
    return pl.BlockSpec((1, 1, N_MOD), lambda b, j: (jnp.where(j == 0, batch, b), 0, 0))


def _inproj_kernel(x_ref, mod_ref, g_ref, w_ref, hg_ref, ml_ref, at_ref, gt_ref):
    x = x_ref[0]
    mod = mod_ref[0]
    sh = mod[:, 0:D_MODEL]
    sc = mod[:, D_MODEL:2 * D_MODEL]
    h = _rms(x) * g_ref[...]
    h = (h * (1.0 + sc) + sh).astype(BF16)
    o0 = 0
    for ref, width in ((hg_ref, HG_IN), (ml_ref, ML_IN), (at_ref, AT_IN), (gt_ref, GATE_W)):
        ref[0] = _dot(h, w_ref[:, o0:o0 + width])
        o0 += width


def _inproj(xs, mod_l, g1, w_in_r):
    batch, rows, _ = xs.shape
    nt = rows // TILE
    n_all = HG_IN + ML_IN + AT_IN + GATE_W
    tok = lambda w: pl.BlockSpec((1, TILE, w), lambda b, j: (b, j, 0))
    return pl.pallas_call(
        _inproj_kernel,
        grid=(batch, nt),
        in_specs=[tok(D_MODEL), _mod_spec(batch),
                  pl.BlockSpec((1, D_MODEL), lambda b, j: (0, 0)),
                  pl.BlockSpec((D_MODEL, n_all), lambda b, j: (0, 0))],
        out_specs=[tok(HG_IN), tok(ML_IN), tok(AT_IN), tok(GATE_W)],
        out_shape=[jax.ShapeDtypeStruct((batch, rows, w), F32)
                   for w in (HG_IN, ML_IN, AT_IN, GATE_W)],
        compiler_params=_cparams(2, VMEM_LIMIT),
        name="inproj",
    )(xs, mod_l, g1, w_in_r)


def _tile_of(d, j, nt):
    return j + d * jnp.where(j > 0, nt - 2 * j, 0)


def _chunk_of(d, i):
    return i + d * (CHUNKS_PER_TILE - 1 - 2 * i)


_HG_LEVELS = (32, 16, 8, 4, 2, 1)


def _head_block_mask():
    h = np.arange(MIX_W) // HEAD_DIM
    return (h[:, None] == h[None, :]).astype(np.float32)


def _split_hi_lo(x):
    hi = x.astype(BF16)
    lo = (x - hi.astype(F32)).astype(BF16)
    return jnp.concatenate([hi, lo], axis=0)


def _block_diag_rows(x, lane_head):
    zero = jnp.zeros_like(x)
    return jnp.concatenate([jnp.where(lane_head == h, x, zero) for h in range(N_HEADS)], axis=0)


@functools.lru_cache(maxsize=None)
def _hgrn2_consts():
    c = CHUNK
    tri = np.tril(np.ones((c, c)))
    ops = [tri]
    masks = []
    pos = np.arange(c)
    for m in _HG_LEVELS:
        blk = pos // (2 * m)
        sel = np.zeros((c, c))
        sel[pos, 2 * m * blk + m - 1] = 1.0
        ops.append(tri - sel @ tri)
        in_b = (pos % (2 * m)) >= m
        masks.append(((blk[:, None] == blk[None, :]) & in_b[:, None] & ~in_b[None, :]).astype(np.float64))
    masks.append(np.eye(c))
    ops.append(np.ones((c, c)) - tri)
    flip = np.eye(c)[::-1]
    stack = lambda xs: np.concatenate(xs, 0)
    ops_d = np.stack([stack(ops), stack([flip @ o @ flip for o in ops])])
    ops_d = np.concatenate([ops_d, ops_d], axis=2)
    tile = lambda mk: np.tile(mk, (1, N_HEADS))
    masks_d = np.stack([np.stack([tile(mk) for mk in masks]), np.stack([tile(mk[::-1, ::-1]) for mk in masks])])
    return ops_d.astype(np.float32), masks_d.astype(np.float32)


def _hgrn2_kernel(q_ref, z_ref, v_ref, lb_ref, ops_ref, msk_ref, hm_ref, o_ref, st_scr, lf_scr, k_scr):
    d = pl.program_id(1)
    j = pl.program_id(2)

    @pl.when(j == 0)
    def _():
        st_scr[...] = jnp.zeros_like(st_scr)

    lb = lb_ref[0]
    z = z_ref[0]
    a = jnp.log(lb)
    cterm = jnp.log1p(-lb) + _log_sigmoid(z)
    lf_scr[...] = jnp.maximum(a, cterm) + jnp.log1p(jnp.exp(-jnp.abs(a - cterm)))
    k_scr[...] = (1.0 - lb) / (1.0 + jnp.exp(z))
    n_lv = len(_HG_LEVELS)
    lane_head = lax.broadcasted_iota(I32, (CHUNK, MIX_W), 1) // HEAD_DIM

    for i in range(CHUNKS_PER_TILE):
        off = pl.multiple_of(_chunk_of(d, i) * CHUNK, CHUNK)
        rows = pl.ds(off, CHUNK)
        cs = _dot(ops_ref[0], _split_hi_lo(lf_scr[rows, :]))
        b = cs[0:CHUNK]
        rest = cs[(n_lv + 1) * CHUNK:(n_lv + 2) * CHUNK]
        q = q_ref[0, rows, :]
        k = k_scr[rows, :]
        vb = v_ref[0, rows, :].astype(BF16)
        q_in = (q * jnp.exp(b)).astype(BF16)
        k_out = (k * jnp.exp(rest)).astype(BF16)
        s_decay = jnp.exp(b[0:1] + rest[0:1])
        sc = jnp.zeros((CHUNK, MIX_W), F32)
        for lv in range(n_lv + 1):
            if lv < n_lv:
                e = jnp.exp(-jnp.abs(cs[(lv + 1) * CHUNK:(lv + 2) * CHUNK]))
                ql = (q * e).astype(BF16)
                kl = (k * e).astype(BF16)
            else:
                ql = q.astype(BF16)
                kl = k.astype(BF16)
            sc = sc + msk_ref[0, lv] * _dot_nt(ql, _block_diag_rows(kl, lane_head))
        st = st_scr[...]
        o_ref[0, 0, rows, :] = (_dot(sc.astype(BF16), _block_diag_rows(vb, lane_head))
                                + _dot_nt(q_in, st.astype(BF16)))
        st_scr[...] = st * s_decay + hm_ref[...] * _dot_tn(vb, k_out)


def _hgrn2(p_hg, lb_l):
    batch, rows, _ = p_hg.shape
    nt = rows // TILE
    ops_d, masks_d = _hgrn2_consts()
    n_ops = ops_d.shape[1]
    n_msk = masks_d.shape[1]
    col = lambda c: pl.BlockSpec((1, TILE, MIX_W), lambda b, d, j: (b, _tile_of(d, j, nt), c))
    return pl.pallas_call(
        _hgrn2_kernel,
        grid=(batch, 2, nt),
        in_specs=[col(0),
                  pl.BlockSpec((1, TILE, MIX_W), lambda b, d, j: (b, _tile_of(d, j, nt), 1 + d)),
                  col(3),
                  pl.BlockSpec((1, 1, MIX_W), lambda b, d, j: (d, 0, 0)),
                  pl.BlockSpec((1, n_ops, 2 * CHUNK), lambda b, d, j: (d, 0, 0)),
                  pl.BlockSpec((1, n_msk, CHUNK, MIX_W), lambda b, d, j: (d, 0, 0, 0)),
                  pl.BlockSpec((MIX_W, MIX_W), lambda b, d, j: (0, 0))],
        out_specs=pl.BlockSpec((1, 1, TILE, MIX_W), lambda b, d, j: (d, b, _tile_of(d, j, nt), 0)),
        out_shape=jax.ShapeDtypeStruct((2, batch, rows, MIX_W), F32),
        scratch_shapes=[pltpu.VMEM((MIX_W, MIX_W), F32),
                        pltpu.VMEM((TILE, MIX_W), F32),
                        pltpu.VMEM((TILE, MIX_W), F32)],
        compiler_params=_cparams(3, VMEM_LIMIT),
        name="hgrn2",
    )(p_hg, p_hg, p_hg, lb_l.reshape(2, 1, MIX_W), jnp.asarray(ops_d, dtype=BF16), jnp.asarray(masks_d),
      jnp.asarray(_head_block_mask()))


@functools.lru_cache(maxsize=None)
def _mlstm_consts():
    c = CHUNK
    tri = np.tril(np.ones((c, c)))
    flip = np.eye(c)[::-1]
    tile = lambda mk: np.tile(mk, (1, N_HEADS))
    ops, masks = [], []
    for t in (tri, flip @ tri @ flip):
        ops.append(np.concatenate([t, t], axis=1))
        masks.append(np.stack([tile(t.T), tile(t), tile(np.eye(c))]))
    return np.stack(ops).astype(np.float32), np.stack(masks).astype(np.float32)


def _mlstm_kernel(q_ref, k_ref, v_ref, gi_ref, gf_ref, bi_ref, bf_ref, op_ref, cm_ref, hm_ref, h_ref,
                  ct_scr, n_scr, m_scr, i_scr, f_scr):
    d = pl.program_id(1)
    j = pl.program_id(2)

    @pl.when(j == 0)
    def _():
        ct_scr[...] = jnp.zeros_like(ct_scr)
        n_scr[...] = jnp.zeros_like(n_scr)
        m_scr[...] = jnp.zeros_like(m_scr)

    i_scr[...] = gi_ref[0] + bi_ref[0]
    f_scr[...] = _log_sigmoid(gf_ref[0] + bf_ref[0])
    lane_head = lax.broadcasted_iota(I32, (CHUNK, MIX_W), 1) // HEAD_DIM
    neg_inf = F32(-jnp.inf)
    hm = hm_ref[...]
    hm_b = hm.astype(BF16)

    for i in range(CHUNKS_PER_TILE):
        off = pl.multiple_of(_chunk_of(d, i) * CHUNK, CHUNK)
        rows = pl.ds(off, CHUNK)
        ie = i_scr[rows, :]
        fe = f_scr[rows, :]
        bc = _dot(op_ref[0], _split_hi_lo(fe))
        row = jnp.sum(ie * cm_ref[0, 2] - fe * cm_ref[0, 0], axis=0, keepdims=True)
        tot = jnp.sum(fe, axis=0, keepdims=True)
        m_st = m_scr[...]
        dm = jnp.where(cm_ref[0, 1] > 0.5, bc + row, neg_inf)
        m_in = jnp.zeros((CHUNK, MIX_W), F32)
        for h in range(N_HEADS):
            mh = jnp.max(dm[:, h * HEAD_DIM:(h + 1) * HEAD_DIM], axis=-1, keepdims=True)
            m_in = jnp.where(lane_head == h, mh, m_in)
        from_prev = bc + m_st
        m_t = jnp.maximum(from_prev, m_in)
        q = q_ref[0, rows, :]
        k = k_ref[0, rows, :] * (HEAD_DIM ** -0.5)
        qb = q.astype(BF16)
        vb = v_ref[0, rows, :].astype(BF16)
        w = jnp.exp(dm - m_t) * _dot_nt(qb, _block_diag_rows(k.astype(BF16), lane_head))
        a_prev = jnp.exp(from_prev - m_t)
        ct = ct_scr[...]
        n_st = n_scr[...]
        num = _dot(w.astype(BF16), _block_diag_rows(vb, lane_head)) + a_prev * _dot_nt(qb, ct.astype(BF16))
        den = _dot((w + a_prev * (q * n_st)).astype(BF16), hm_b)
        h_ref[0, 0, rows, :] = num / jnp.maximum(jnp.abs(den), jnp.exp(-m_t))
        e_col = tot - bc + ie
        m_new = jnp.maximum(tot + m_st, jnp.max(e_col, axis=0, keepdims=True))
        a_end = jnp.exp(tot + m_st - m_new)
        kw = k * jnp.exp(e_col - m_new)
        ct_scr[...] = ct * a_end + hm * _dot_tn(vb, kw.astype(BF16))
        n_scr[...] = a_end * n_st + jnp.sum(kw, axis=0, keepdims=True)
        m_scr[...] = m_new


def _mlstm(p_ml, p_gt, bias_i, bias_f):
    batch, rows, _ = p_ml.shape
    nt = rows // TILE
    ops_d, masks_d = _mlstm_consts()
    col = lambda c: pl.BlockSpec((1, TILE, MIX_W), lambda b, d, j: (b, _tile_of(d, j, nt), c))
    gate = lambda kind: pl.BlockSpec((1, TILE, MIX_W), lambda b, d, j: (b, _tile_of(d, j, nt), 2 * d + kind))
    by_dir = lambda shp: pl.BlockSpec((1,) + shp, lambda b, d, j: (d,) + (0,) * len(shp))
    return pl.pallas_call(
        _mlstm_kernel,
        grid=(batch, 2, nt),
        in_specs=[col(0), col(1), col(2), gate(0), gate(1),
                  by_dir((1, MIX_W)), by_dir((1, MIX_W)),
                  by_dir((CHUNK, 2 * CHUNK)), by_dir((3, CHUNK, MIX_W)),
                  pl.BlockSpec((MIX_W, MIX_W), lambda b, d, j: (0, 0))],
        out_specs=pl.BlockSpec((1, 1, TILE, MIX_W), lambda b, d, j: (d, b, _tile_of(d, j, nt), 0)),
        out_shape=jax.ShapeDtypeStruct((2, batch, rows, MIX_W), F32),
        scratch_shapes=[pltpu.VMEM((MIX_W, MIX_W), F32),
                        pltpu.VMEM((1, MIX_W), F32),
                        pltpu.VMEM((1, MIX_W), F32),
                        pltpu.VMEM((TILE, MIX_W), F32),
                        pltpu.VMEM((TILE, MIX_W), F32)],
        compiler_params=_cparams(3, VMEM_LIMIT),
        name="mlstm",
    )(p_ml, p_ml, p_ml, p_gt, p_gt, bias_i, bias_f, jnp.asarray(ops_d, dtype=BF16), jnp.asarray(masks_d),
      jnp.asarray(_head_block_mask()))


def _rope_tables(rows, ctx_len):
    n = rows - ctx_len
    r = jnp.arange(n) // GRID_W
    cc = jnp.arange(n) % GRID_W
    axis_dims = AT_HEAD_DIM // 2
    inv_freq = ROPE_THETA ** (-jnp.arange(0, axis_dims, 2, dtype=F32) / axis_dims)
    ang_r = r.astype(F32)[:, None] * inv_freq
    ang_c = cc.astype(F32)[:, None] * inv_freq
    cos = jnp.concatenate([jnp.cos(ang_r), jnp.cos(ang_r), jnp.cos(ang_c), jnp.cos(ang_c)], -1)
    sin = jnp.concatenate([-jnp.sin(ang_r), jnp.sin(ang_r), -jnp.sin(ang_c), jnp.sin(ang_c)], -1)
    cos = jnp.concatenate([jnp.ones((ctx_len, AT_HEAD_DIM), F32), cos], 0)
    sin = jnp.concatenate([jnp.zeros((ctx_len, AT_HEAD_DIM), F32), sin], 0)
    return cos, sin


def _attn_prep_kernel(p_ref, cos_ref, sin_ref, qg_ref, kg_ref, q_ref, k_ref, v_ref):
    cos = cos_ref[...]
    sin = sin_ref[...]
    lane = lax.broadcasted_iota(I32, (TILE, AT_HEAD_DIM), 1)
    first_half = (lane % (AT_HEAD_DIM // 2)) < (AT_HEAD_DIM // 4)
    quarter = AT_HEAD_DIM // 4

    def norm_rope(x, g):
        y = _rms(x) * g
        partner = jnp.where(first_half, pltpu.roll(y, AT_HEAD_DIM - quarter, axis=1), pltpu.roll(y, quarter, axis=1))
        return y * cos + partner * sin

    scale = AT_HEAD_DIM ** -0.5
    for h in range(AT_HEADS):
        sl = slice(h * AT_HEAD_DIM, (h + 1) * AT_HEAD_DIM)
        q_ref[0, :, sl] = (norm_rope(p_ref[0, :, sl], qg_ref[...]) * scale).astype(BF16)
    for h in range(AT_KV_HEADS):
        sl = slice(h * AT_HEAD_DIM, (h + 1) * AT_HEAD_DIM)
        src = slice((AT_HEADS + h) * AT_HEAD_DIM, (AT_HEADS + h + 1) * AT_HEAD_DIM)
        k_ref[0, :, sl] = norm_rope(p_ref[0, :, src], kg_ref[...]).astype(BF16)
    for h in range(AT_KV_HEADS):
        src = slice((AT_HEADS + AT_KV_HEADS + h) * AT_HEAD_DIM, (AT_HEADS + AT_KV_HEADS + h + 1) * AT_HEAD_DIM)
        v_ref[0, :, 2 * h * AT_HEAD_DIM:(2 * h + 1) * AT_HEAD_DIM] = p_ref[0, :, src].astype(BF16)
        v_ref[0, :, (2 * h + 1) * AT_HEAD_DIM:(2 * h + 2) * AT_HEAD_DIM] = jnp.ones((TILE, AT_HEAD_DIM), BF16)


def _attn_prep(p_at, cos, sin, q_g, k_g):
    batch, rows, _ = p_at.shape
    nt = rows // TILE
    tok = lambda w: pl.BlockSpec((1, TILE, w), lambda b, j: (b, j, 0))
    tab = pl.BlockSpec((TILE, AT_HEAD_DIM), lambda b, j: (j, 0))
    vec = pl.BlockSpec((1, AT_HEAD_DIM), lambda b, j: (0, 0))
    kvw = AT_KV_HEADS * AT_HEAD_DIM
    return pl.pallas_call(
        _attn_prep_kernel,
        grid=(batch, nt),
        in_specs=[tok(AT_IN), tab, tab, vec, vec],
        out_specs=[tok(AT_HEADS * AT_HEAD_DIM), tok(kvw), tok(2 * kvw)],
        out_shape=[jax.ShapeDtypeStruct((batch, rows, AT_HEADS * AT_HEAD_DIM), BF16),
                   jax.ShapeDtypeStruct((batch, rows, kvw), BF16),
                   jax.ShapeDtypeStruct((batch, rows, 2 * kvw), BF16)],
        compiler_params=_cparams(2, VMEM_LIMIT),
        name="attn_prep",
    )(p_at, cos, sin, q_g, k_g)


def _attn_kernel(q_ref, k_ref, v_ref, o_ref):
    j = pl.program_id(2)

    def attend(n_keys):
        q2 = jnp.concatenate([q_ref[0, :, g * AT_HEAD_DIM:(g + 1) * AT_HEAD_DIM] for g in range(AT_GROUP)], axis=0)
        m = jnp.full((AT_GROUP * TILE, 1), -jnp.inf, F32)
        acc = jnp.zeros((AT_GROUP * TILE, 2 * AT_HEAD_DIM), F32)
        bounds = list(range(0, n_keys, ATT_KEY_CHUNK))
        bounds = bounds[:max(len(bounds) - 1, 1)] if n_keys % ATT_KEY_CHUNK else bounds
        bounds.append(n_keys)
        for c0, c1 in zip(bounds[:-1], bounds[1:]):
            s = _dot_nt(q2, k_ref[0, c0:c1, :])
            m_new = jnp.maximum(m, jnp.max(s, axis=-1, keepdims=True))
            p = jnp.exp((s - m_new).astype(BF16))
            acc = acc * jnp.exp(m - m_new) + _dot(p, v_ref[0, c0:c1, :])
            m = m_new
        o = acc[:, 0:AT_HEAD_DIM] / acc[:, AT_HEAD_DIM:AT_HEAD_DIM + 1]
        for g in range(AT_GROUP):
            o_ref[0, :, g * AT_HEAD_DIM:(g + 1) * AT_HEAD_DIM] = o[g * TILE:(g + 1) * TILE]

    @pl.when(j == 0)
    def _():
        attend(TILE)

    @pl.when(j > 0)
    def _():
        attend(k_ref.shape[1])


def _attention(qn, kn, vb):
    batch, rows, _ = qn.shape
    nt = rows // TILE
    gw = AT_GROUP * AT_HEAD_DIM
    return pl.pallas_call(
        _attn_kernel,
        grid=(batch, AT_KV_HEADS, nt),
        in_specs=[pl.BlockSpec((1, TILE, gw), lambda b, h, j: (b, j, h)),
                  pl.BlockSpec((1, rows, AT_HEAD_DIM), lambda b, h, j: (b, 0, h)),
                  pl.BlockSpec((1, rows, 2 * AT_HEAD_DIM), lambda b, h, j: (b, 0, h))],
        out_specs=pl.BlockSpec((1, TILE, gw), lambda b, h, j: (b, j, h)),
        out_shape=jax.ShapeDtypeStruct((batch, rows, AT_HEADS * AT_HEAD_DIM), F32),
        compiler_params=_cparams(3, VMEM_LIMIT),
        name="attention",
    )(qn, kn, vb)


def _outproj_kernel(x_ref, mod_ref, ohf_ref, ohb_ref, gh_ref, omf_ref, omb_ref, gm_ref, at_ref,
                    hgg_ref, mlg_ref, ones_ref, wo_ref, g2_ref, wr_ref, br_ref, low_ref,
                    xo_ref, h2_ref, idx_ref, gate_ref, cnt_ref, cnt_scr):
    b = pl.program_id(0)
    j = pl.program_id(1)

    @pl.when((b == 0) & (j == 0))
    def _():
        cnt_scr[...] = jnp.zeros_like(cnt_scr)

    mod = mod_ref[0]
    gate1 = mod[:, 2 * D_MODEL:3 * D_MODEL]
    sh2 = mod[:, 3 * D_MODEL:4 * D_MODEL]
    sc2 = mod[:, 4 * D_MODEL:5 * D_MODEL]

    def head_norm(o, g):
        ms = _dot((o * o).astype(BF16), ones_ref[...]) * (1.0 / HEAD_DIM)
        return o * lax.rsqrt(ms + NORM_EPS) * g

    gh = gh_ref[0]
    y_hg = head_norm(ohf_ref[0, 0] + ohb_ref[0, 0], hgg_ref[...]) * (gh / (1.0 + jnp.exp(-gh)))
    y_ml = head_norm(omf_ref[0, 0] + omb_ref[0, 0], mlg_ref[...]) / (1.0 + jnp.exp(-gm_ref[0]))
    proj = (_dot(y_hg.astype(BF16), wo_ref[0:MIX_W, :])
            + _dot(y_ml.astype(BF16), wo_ref[MIX_W:2 * MIX_W, :])
            + _dot(at_ref[0].astype(BF16), wo_ref[2 * MIX_W:, :]))
    x = x_ref[0] + gate1 * proj
    xo_ref[0] = x
    h2 = _rms(x) * g2_ref[...]
    h2 = h2 * (1.0 + sc2) + sh2
    h2_ref[...] = h2.reshape(TILE, ROW_SUB, LANES).astype(BF16)

    lane = lax.broadcasted_iota(I32, (TILE, LANES), 1)
    lane_f = lane.astype(F32)
    h_hi = h2.astype(BF16)
    h_lo = (h2 - h_hi.astype(F32)).astype(BF16)
    logits = _dot(jnp.concatenate([h_hi, h_lo, h_hi], axis=1), wr_ref[...]) + br_ref[...]
    lg = jnp.where(lane < N_EXPERTS, logits, -jnp.inf)
    multi = jnp.zeros((TILE, LANES), F32)
    hots, vals, ids = [], [], []
    for _ in range(TOP_K):
        mx = jnp.max(lg, axis=-1, keepdims=True)
        idx = jnp.min(jnp.where(lg == mx, lane_f, F32(LANES)), axis=-1, keepdims=True)
        hot = lane_f == idx
        lg = jnp.where(hot, -jnp.inf, lg)
        multi = multi + hot.astype(F32)
        hots.append(hot)
        vals.append(mx)
        ids.append(idx.astype(I32))
    es = [jnp.exp(v - vals[0]) for v in vals]
    den = es[0] + es[1] + es[2] + es[3]
    before = _dot(low_ref[...], multi.astype(BF16)) + cnt_scr[...]
    idx_out = jnp.zeros((TILE, LANES), I32)
    gate_out = jnp.zeros((TILE, LANES), F32)
    for kk in range(TOP_K):
        rank = jnp.sum(jnp.where(hots[kk], before, 0.0), axis=-1, keepdims=True).astype(I32)
        idx_out = idx_out + jnp.where(lane == kk, ids[kk], 0) + jnp.where(lane == TOP_K + kk, rank, 0)
        gate_out = gate_out + jnp.where(lane == kk, es[kk] / den, 0.0)
    idx_ref[0] = idx_out[:, 0:2 * TOP_K]
    gate_ref[0] = gate_out
    cnt_scr[...] = cnt_scr[...] + jnp.sum(multi, axis=0, keepdims=True)
    cnt_ref[...] = cnt_scr[...]


def _outproj(xs, mod_l, o_hg, p_hg, o_ml, p_ml, att, hg_g, ml_g, w_out_b, g2, w_router_p, b_router_p):
    batch, rows, _ = xs.shape
    nt = rows // TILE
    tok = lambda w, c=0: pl.BlockSpec((1, TILE, w), lambda b, j: (b, j, c))
    dirs = lambda d: pl.BlockSpec((1, 1, TILE, MIX_W), lambda b, j: (d, b, j, 0))
    full = lambda r, c: pl.BlockSpec((r, c), lambda b, j: (0, 0))
    block_ones = np.kron(np.eye(N_HEADS), np.ones((HEAD_DIM, HEAD_DIM))).astype(np.float32)
    strict_low = np.tril(np.ones((TILE, TILE)), -1).astype(np.float32)
    return pl.pallas_call(
        _outproj_kernel,
        grid=(batch, nt),
        in_specs=[tok(D_MODEL), _mod_spec(batch),
                  dirs(0), dirs(1), tok(MIX_W, 4),
                  dirs(0), dirs(1), tok(MIX_W, 3),
                  tok(AT_HEADS * AT_HEAD_DIM),
                  full(1, MIX_W), full(1, MIX_W), full(MIX_W, MIX_W),
                  full(D_MODEL, D_MODEL), full(1, D_MODEL),
                  full(3 * D_MODEL, LANES), full(1, LANES), full(TILE, TILE)],
        out_specs=[tok(D_MODEL),
                   pl.BlockSpec((TILE, ROW_SUB, LANES), lambda b, j: (b * nt + j, 0, 0)),
                   tok(2 * TOP_K), tok(LANES),
                   pl.BlockSpec((1, LANES), lambda b, j: (0, 0))],
        out_shape=[jax.ShapeDtypeStruct((batch, rows, D_MODEL), F32),
                   jax.ShapeDtypeStruct((batch * rows, ROW_SUB, LANES), BF16),
                   jax.ShapeDtypeStruct((batch, rows, 2 * TOP_K), I32),
                   jax.ShapeDtypeStruct((batch, rows, LANES), F32),
                   jax.ShapeDtypeStruct((1, LANES), F32)],
        scratch_shapes=[pltpu.VMEM((1, LANES), F32)],
        compiler_params=_cparams(2, VMEM_LIMIT),
        name="outproj_router",
    )(xs, mod_l, o_hg, o_hg, p_hg, o_ml, o_ml, p_ml, att, hg_g, ml_g, jnp.asarray(block_ones, dtype=BF16),
      w_out_b, g2, w_router_p, b_router_p, jnp.asarray(strict_low, dtype=BF16))


def _slot(pstart_ref, idx_ref, r, kk):
    base = r * (2 * TOP_K) + kk
    return pstart_ref[idx_ref[0, 0, base]] + idx_ref[0, 0, base + TOP_K]


def _dispatch_kernel(pstart_ref, idx_ref, h_ref, zero_ref, o_ref, sem):
    del zero_ref

    def row_copy(r, kk):
        return pltpu.make_async_copy(h_ref.at[r], o_ref.at[_slot(pstart_ref, idx_ref, r, kk)], sem)

    def issue(r, carry):
        for kk in range(TOP_K):
            row_copy(r, kk).start()
        return carry

    lax.fori_loop(0, TILE, issue, 0)

    def drain(r, carry):
        for kk in range(TOP_K):
            row_copy(r, kk).wait()
        return carry

    lax.fori_loop(0, TILE, drain, 0)


def _dispatch(pstart, idx_t, h2, n_slots):
    n_tiles = idx_t.shape[0]
    zeros = jnp.zeros((n_slots, ROW_SUB, LANES), BF16)
    grid_spec = pltpu.PrefetchScalarGridSpec(
        num_scalar_prefetch=1,
        grid=(n_tiles,),
        in_specs=[pl.BlockSpec((1, 1, TILE * 2 * TOP_K), lambda t, ps: (t, 0, 0), memory_space=pltpu.SMEM),
                  pl.BlockSpec((TILE, ROW_SUB, LANES), lambda t, ps: (t, 0, 0)),
                  pl.BlockSpec(memory_space=pl.ANY)],
        out_specs=pl.BlockSpec(memory_space=pl.ANY),
        scratch_shapes=[pltpu.SemaphoreType.DMA(())],
    )
    return pl.pallas_call(
        _dispatch_kernel,
        grid_spec=grid_spec,
        out_shape=jax.ShapeDtypeStruct((n_slots, ROW_SUB, LANES), BF16),
        input_output_aliases={3: 0},
        compiler_params=_cparams(1, VMEM_LIMIT),
        name="moe_dispatch",
    )(pstart, idx_t, h2, zeros)


def _expert_kernel(be_ref, nv_ref, x_ref, wgu_ref, bgu_ref, wd_ref, bd_ref, y_ref, wgu_scr, wd_scr):
    i = pl.program_id(0)

    @pl.when(i < nv_ref[0])
    def _():
        prev = be_ref[jnp.maximum(i - 1, 0)]

        @pl.when((i == 0) | (prev != be_ref[i]))
        def _():
            wgu_scr[...] = wgu_ref[0].astype(BF16)
            wd_scr[...] = wd_ref[0].astype(BF16)

        x = x_ref[...].reshape(MOE_BLOCK, D_MODEL)
        gu = _dot(x, wgu_scr[...]) + bgu_ref[0]
        gate = jnp.minimum(gu[:, 0:D_EXPERT], SWIGLU_LIMIT)
        up = jnp.clip(gu[:, D_EXPERT:], -SWIGLU_LIMIT, SWIGLU_LIMIT)
        act = (up + 1.0) * gate * (1.0 / (1.0 + jnp.exp(-SWIGLU_ALPHA * gate)))
        y = _dot(act.astype(BF16), wd_scr[...]) + bd_ref[0]
        y_ref[...] = y.reshape(MOE_BLOCK, ROW_SUB, LANES)

    @pl.when(i >= nv_ref[0])
    def _():
        y_ref[...] = jnp.zeros_like(y_ref)


def _experts(block_e, n_valid, xs_sorted, w_gu, b_gu, w_down, b_down):
    n_slots = xs_sorted.shape[0]
    nb = n_slots // MOE_BLOCK
    clamp = lambda i, nv: jnp.minimum(i, jnp.maximum(nv[0] - 1, 0))
    grid_spec = pltpu.PrefetchScalarGridSpec(
        num_scalar_prefetch=2,
        grid=(nb,),
        in_specs=[pl.BlockSpec((MOE_BLOCK, ROW_SUB, LANES), lambda i, be, nv: (clamp(i, nv), 0, 0)),
                  pl.BlockSpec((1, D_MODEL, 2 * D_EXPERT), lambda i, be, nv: (be[i], 0, 0)),
                  pl.BlockSpec((1, 1, 2 * D_EXPERT), lambda i, be, nv: (be[i], 0, 0)),
                  pl.BlockSpec((1, D_EXPERT, D_MODEL), lambda i, be, nv: (be[i], 0, 0)),
                  pl.BlockSpec((1, 1, D_MODEL), lambda i, be, nv: (be[i], 0, 0))],
        out_specs=pl.BlockSpec((MOE_BLOCK, ROW_SUB, LANES), lambda i, be, nv: (i, 0, 0)),
        scratch_shapes=[pltpu.VMEM((D_MODEL, 2 * D_EXPERT), BF16),
                        pltpu.VMEM((D_EXPERT, D_MODEL), BF16)],
    )
    return pl.pallas_call(
        _expert_kernel,
        grid_spec=grid_spec,
        out_shape=jax.ShapeDtypeStruct((n_slots, ROW_SUB, LANES), F32),
        compiler_params=_cparams(1, VMEM_LIMIT),
        name="moe_experts",
    )(block_e, n_valid, xs_sorted, w_gu, b_gu.reshape(N_EXPERTS, 1, 2 * D_EXPERT),
      w_down, b_down.reshape(N_EXPERTS, 1, D_MODEL))


def _combine_kernel(pstart_ref, idx_ref, x_ref, mod_ref, gate_ref, y_ref, o_ref, buf, sem):
    def row_copy(r, kk):
        return pltpu.make_async_copy(y_ref.at[_slot(pstart_ref, idx_ref, r, kk)], buf.at[kk, r], sem)

    def issue(r, carry):
        for kk in range(TOP_K):
            row_copy(r, kk).start()
        return carry

    lax.fori_loop(0, TILE, issue, 0)

    def drain(r, carry):
        for kk in range(TOP_K):
            row_copy(r, kk).wait()
        return carry

    lax.fori_loop(0, TILE, drain, 0)

    gates = gate_ref[0]
    f = jnp.zeros((TILE, D_MODEL), F32)
    for kk in range(TOP_K):
        f = f + buf[kk].reshape(TILE, D_MODEL) * gates[:, kk:kk + 1]
    gate2 = mod_ref[0][:, 5 * D_MODEL:6 * D_MODEL]
    o_ref[0] = x_ref[0] + gate2 * f


def _combine(pstart, idx_t, xs, mod_l, gates, y_sorted):
    batch, rows, _ = xs.shape
    nt = rows // TILE
    mod_spec = _mod_spec(batch)
    grid_spec = pltpu.PrefetchScalarGridSpec(
        num_scalar_prefetch=1,
        grid=(batch, nt),
        in_specs=[pl.BlockSpec((1, 1, TILE * 2 * TOP_K), lambda b, j, ps: (b * nt + j, 0, 0),
                               memory_space=pltpu.SMEM),
                  pl.BlockSpec((1, TILE, D_MODEL), lambda b, j, ps: (b, j, 0)),
                  pl.BlockSpec(mod_spec.block_shape, lambda b, j, ps: mod_spec.index_map(b, j)),
                  pl.BlockSpec((1, TILE, LANES), lambda b, j, ps: (b, j, 0)),
                  pl.BlockSpec(memory_space=pl.ANY)],
        out_specs=pl.BlockSpec((1, TILE, D_MODEL), lambda b, j, ps: (b, j, 0)),
        scratch_shapes=[pltpu.VMEM((TOP_K, TILE, ROW_SUB, LANES), F32),
                        pltpu.SemaphoreType.DMA(())],
    )
    return pl.pallas_call(
        _combine_kernel,
        grid_spec=grid_spec,
        out_shape=jax.ShapeDtypeStruct((batch, rows, D_MODEL), F32),
        compiler_params=_cparams(2, VMEM_LIMIT),
        name="moe_combine",
    )(pstart, idx_t, xs, mod_l, gates, y_sorted)


def _moe_plan(counts, n_tokens):
    cnt = counts[0, :N_EXPERTS].astype(I32)
    padded = (cnt + MOE_BLOCK - 1) // MOE_BLOCK * MOE_BLOCK
    pend = jnp.cumsum(padded)
    pstart = pend - padded
    nb = -(-(n_tokens * TOP_K) // MOE_BLOCK) + N_EXPERTS
    block_row = jnp.arange(nb, dtype=I32) * MOE_BLOCK
    block_e = jnp.minimum(jnp.sum((pend[None, :] <= block_row[:, None]).astype(I32), axis=1), N_EXPERTS - 1)
    n_valid = (pend[-1] // MOE_BLOCK).astype(I32).reshape(1)
    return pstart.astype(I32), block_e.astype(I32), n_valid, nb * MOE_BLOCK


def _arrange_w_in(w_in_l):
    hg = w_in_l[:, 0:HG_IN]
    ml0 = HG_IN
    ml = w_in_l[:, ml0:ml0 + ML_IN]
    gt = w_in_l[:, ml0 + ML_IN:ml0 + ML_IN + 4 * N_HEADS]
    at = w_in_l[:, ml0 + ML_IN + 4 * N_HEADS:]
    gt = jnp.repeat(_gate_order(gt.T).T, HEAD_DIM, axis=1)
    return jnp.concatenate([hg, ml, at, gt], 1).astype(BF16)


def _gate_order(g):
    g = g.reshape((4, N_HEADS) + g.shape[1:]) if g.shape[0] == 4 * N_HEADS else g
    return jnp.concatenate([g[0], g[2], g[1], g[3]], axis=0)


def _hgrn2_lower_bounds(lb_logits):
    p = jax.nn.softmax(lb_logits.astype(F32), axis=1)
    cum = jnp.cumsum(p, axis=1)
    return cum - cum[:, :1]


def kernel(x, c, ctx, c_ctx, w_mod, b_mod, norm1_g, norm2_g, w_in, w_out, hg_lb_logits, hg_norm_g,
           ml_gate_bias, ml_norm_g, q_norm_g, k_norm_g, w_router, b_router, w_gu, b_gu, w_down, b_down):
    batch, seq, d = x.shape
    ctx_len = ctx.shape[1]
    depth = w_mod.shape[0]
    assert d == D_MODEL and ctx_len == TILE and seq % TILE == 0 and batch + 1 <= MOD_ROWS
    rows = ctx_len + seq
    n_tokens = batch * rows

    cvec = jnp.concatenate([c, c_ctx[None, :], jnp.zeros((MOD_ROWS - batch - 1, d), F32)], 0)
    mod = _modulation(cvec, w_mod, b_mod).reshape(depth, MOD_ROWS, 1, N_MOD)
    hg_lb = _hgrn2_lower_bounds(hg_lb_logits)
    cos, sin = _rope_tables(rows, ctx_len)
    xs = jnp.concatenate([ctx, x], axis=1)

    for layer in range(depth):
        mod_l = mod[layer]
        p_hg, p_ml, p_at, p_gt = _inproj(xs, mod_l, norm1_g[layer][None, :], _arrange_w_in(w_in[layer]))
        o_hg = _hgrn2(p_hg, hg_lb[:, layer])
        gb = jnp.repeat(_gate_order(ml_gate_bias[layer]), HEAD_DIM).reshape(4, 1, MIX_W)
        o_ml = _mlstm(p_ml, p_gt, gb[0::2], gb[1::2])
        qn, kn, vb = _attn_prep(p_at, cos, sin, q_norm_g[layer][None, :], k_norm_g[layer][None, :])
        att = _attention(qn, kn, vb)
        w_router_p = jnp.pad(w_router[layer], ((0, 0), (0, LANES - N_EXPERTS)))
        wr_hi = w_router_p.astype(BF16)
        wr_lo = (w_router_p - wr_hi.astype(F32)).astype(BF16)
        w_router_p = jnp.concatenate([wr_hi, wr_hi, wr_lo], axis=0)
        b_router_p = jnp.pad(b_router[layer], (0, LANES - N_EXPERTS))[None, :]
        xs, h2, idx, gates, counts = _outproj(
            xs, mod_l, o_hg, p_hg, o_ml, p_ml, att,
            jnp.tile(hg_norm_g[layer], N_HEADS)[None, :], jnp.tile(ml_norm_g[layer], N_HEADS)[None, :],
            w_out[layer].astype(BF16), norm2_g[layer][None, :], w_router_p, b_router_p)
        pstart, block_e, n_valid, n_slots = _moe_plan(counts, n_tokens)
        idx_t = idx.reshape(n_tokens // TILE, 1, TILE * 2 * TOP_K)
        xs_sorted = _dispatch(pstart, idx_t, h2, n_slots)
        y_sorted = _experts(block_e, n_valid, xs_sorted, w_gu[layer], b_gu[layer], w_down[layer], b_down[layer])
        xs = _combine(pstart, idx_t, xs, mod_l, gates, y_sorted)
    return xs[:, ctx_len:, :]
```

```python
import functools

import numpy as np
import jax
import jax.numpy as jnp
from jax import lax
from jax.experimental import pallas as pl
from jax.experimental.pallas import tpu as pltpu

F32 = jnp.float32
BF16 = jnp.bfloat16
I32 = jnp.int32
HIGHEST = lax.Precision.HIGHEST

D_MODEL = 1024
NORM_EPS = 1e-6
CHUNK = 64
TILE = 256
CHUNKS_PER_TILE = TILE // CHUNK
GRID_W = 64
ROPE_THETA = 10000.0

N_HEADS = 4
HEAD_DIM = 64
MIX_W = N_HEADS * HEAD_DIM
AT_HEADS = 4
AT_KV_HEADS = 2
AT_HEAD_DIM = 128
AT_GROUP = AT_HEADS // AT_KV_HEADS
ATT_KEY_CHUNK = 1024

HG_IN = 5 * MIX_W
ML_IN = 4 * MIX_W
AT_IN = (AT_HEADS + 2 * AT_KV_HEADS) * AT_HEAD_DIM
GATE_W = 4 * MIX_W

N_EXPERTS = 32
TOP_K = 4
D_EXPERT = 1024
SWIGLU_LIMIT = 7.0
SWIGLU_ALPHA = 1.702
MOE_BLOCK = 512
ROW_SUB = D_MODEL // 128
LANES = 128
N_MOD = 6 * D_MODEL
MOD_ROWS = 16

VMEM_LIMIT = 56 * 1024 * 1024


def _cparams(n_axes, vmem=None):
    kw = dict(dimension_semantics=("arbitrary",) * n_axes)
    if vmem is not None:
        kw["vmem_limit_bytes"] = vmem
    return pltpu.CompilerParams(**kw)


def _dot(a, b, precision=None):
    return jnp.dot(a, b, preferred_element_type=F32, precision=precision)


def _dot_nt(a, b):
    return lax.dot_general(a, b, (((1,), (1,)), ((), ())), preferred_element_type=F32)


def _dot_tn(a, b, precision=None):
    return lax.dot_general(a, b, (((0,), (0,)), ((), ())), preferred_element_type=F32,
                           precision=precision)


def _log_sigmoid(z):
    return jnp.minimum(z, 0.0) - jnp.log1p(jnp.exp(-jnp.abs(z)))


def _rms(x, eps=NORM_EPS):
    return x * lax.rsqrt(jnp.mean(x * x, axis=-1, keepdims=True) + eps)


def _mod_kernel(c_ref, w_ref, b_ref, o_ref):
    cv = c_ref[...]
    s = cv * (1.0 / (1.0 + jnp.exp(-cv)))
    o_ref[0] = _dot(s, w_ref[0], precision=HIGHEST) + b_ref[0]


def _modulation(cvec, w_mod, b_mod):
    depth = w_mod.shape[0]
    nblk = 1536
    return pl.pallas_call(
        _mod_kernel,
        grid=(depth, N_MOD // nblk),
        in_specs=[pl.BlockSpec((MOD_ROWS, D_MODEL), lambda l, n: (0, 0)),
                  pl.BlockSpec((1, D_MODEL, nblk), lambda l, n: (l, 0, n)),
                  pl.BlockSpec((1, 1, nblk), lambda l, n: (l, 0, n))],
        out_specs=pl.BlockSpec((1, MOD_ROWS, nblk), lambda l, n: (l, 0, n)),
        out_shape=jax.ShapeDtypeStruct((depth, MOD_ROWS, N_MOD), F32),
        compiler_params=_cparams(2, VMEM_LIMIT),
        name="modulation",
    )(cvec, w_mod, b_mod.reshape(depth, 1, N_MOD))


def _mod_spec(batch):
    return pl.BlockSpec((1, 1, N_MOD), lambda b, j: (jnp.where(j == 0, batch, b), 0, 0))


def _inproj_kernel(x_ref, mod_ref, g_ref, w_ref, hg_ref, ml_ref, at_ref, gt_ref):
    x = x_ref[0]
    mod = mod_ref[0]
    sh = mod[:, 0:D_MODEL]
    sc = mod[:, D_MODEL:2 * D_MODEL]
    h = _rms(x) * g_ref[...]
    h = (h * (1.0 + sc) + sh).astype(BF16)
    o0 = 0
    for ref, width in ((hg_ref, HG_IN), (ml_ref, ML_IN), (at_ref, AT_IN), (gt_ref, GATE_W)):
        ref[0] = _dot(h, w_ref[:, o0:o0 + width])
        o0 += width


def _inproj(xs, mod_l, g1, w_in_r):
    batch, rows, _ = xs.shape
    nt = rows // TILE
    n_all = HG_IN + ML_IN + AT_IN + GATE_W
    tok = lambda w: pl.BlockSpec((1, TILE, w), lambda b, j: (b, j, 0))
    return pl.pallas_call(
        _inproj_kernel,
        grid=(batch, nt),
        in_specs=[tok(D_MODEL), _mod_spec(batch),
                  pl.BlockSpec((1, D_MODEL), lambda b, j: (0, 0)),
                  pl.BlockSpec((D_MODEL, n_all), lambda b, j: (0, 0))],
        out_specs=[tok(HG_IN), tok(ML_IN), tok(AT_IN), tok(GATE_W)],
        out_shape=[jax.ShapeDtypeStruct((batch, rows, w), F32)
                   for w in (HG_IN, ML_IN, AT_IN, GATE_W)],
        compiler_params=_cparams(2, VMEM_LIMIT),
        name="inproj",
    )(xs, mod_l, g1, w_in_r)


def _tile_of(d, j, nt):
    return j + d * jnp.where(j > 0, nt - 2 * j, 0)


def _chunk_of(d, i):
    return i + d * (CHUNKS_PER_TILE - 1 - 2 * i)


_HG_LEVELS = (32, 16, 8, 4, 2, 1)


def _head_block_mask():
    h = np.arange(MIX_W) // HEAD_DIM
    return (h[:, None] == h[None, :]).astype(np.float32)


def _split_hi_lo(x):
    hi = x.astype(BF16)
    lo = (x - hi.astype(F32)).astype(BF16)
    return jnp.concatenate([hi, lo], axis=0)


def _block_diag_rows(x, lane_head):
    zero = jnp.zeros_like(x)
    return jnp.concatenate([jnp.where(lane_head == h, x, zero) for h in range(N_HEADS)], axis=0)


@functools.lru_cache(maxsize=None)
def _hgrn2_consts():
    c = CHUNK
    tri = np.tril(np.ones((c, c)))
    ops = [tri]
    masks = []
    pos = np.arange(c)
    for m in _HG_LEVELS:
        blk = pos // (2 * m)
        sel = np.zeros((c, c))
        sel[pos, 2 * m * blk + m - 1] = 1.0
        ops.append(tri - sel @ tri)
        in_b = (pos % (2 * m)) >= m
        masks.append(((blk[:, None] == blk[None, :]) & in_b[:, None] & ~in_b[None, :]).astype(np.float64))
    masks.append(np.eye(c))
    ops.append(np.ones((c, c)) - tri)
    flip = np.eye(c)[::-1]
    stack = lambda xs: np.concatenate(xs, 0)
    ops_d = np.stack([stack(ops), stack([flip @ o @ flip for o in ops])])
    ops_d = np.concatenate([ops_d, ops_d], axis=2)
    tile = lambda mk: np.tile(mk, (1, N_HEADS))
    masks_d = np.stack([np.stack([tile(mk) for mk in masks]), np.stack([tile(mk[::-1, ::-1]) for mk in masks])])
    return ops_d.astype(np.float32), masks_d.astype(np.float32)


def _hgrn2_kernel(q_ref, z_ref, v_ref, lb_ref, ops_ref, msk_ref, hm_ref, o_ref, st_scr, lf_scr, k_scr):
    d = pl.program_id(1)
    j = pl.program_id(2)

    @pl.when(j == 0)
    def _():
        st_scr[...] = jnp.zeros_like(st_scr)

    lb = lb_ref[0]
    z = z_ref[0]
    a = jnp.log(lb)
    cterm = jnp.log1p(-lb) + _log_sigmoid(z)
    lf_scr[...] = jnp.maximum(a, cterm) + jnp.log1p(jnp.exp(-jnp.abs(a - cterm)))
    k_scr[...] = (1.0 - lb) / (1.0 + jnp.exp(z))
    n_lv = len(_HG_LEVELS)
    lane_head = lax.broadcasted_iota(I32, (CHUNK, MIX_W), 1) // HEAD_DIM

    for i in range(CHUNKS_PER_TILE):
        off = pl.multiple_of(_chunk_of(d, i) * CHUNK, CHUNK)
        rows = pl.ds(off, CHUNK)
        cs = _dot(ops_ref[0], _split_hi_lo(lf_scr[rows, :]))
        b = cs[0:CHUNK]
        rest = cs[(n_lv + 1) * CHUNK:(n_lv + 2) * CHUNK]
        q = q_ref[0, rows, :]
        k = k_scr[rows, :]
        vb = v_ref[0, rows, :].astype(BF16)
        q_in = (q * jnp.exp(b)).astype(BF16)
        k_out = (k * jnp.exp(rest)).astype(BF16)
        s_decay = jnp.exp(b[0:1] + rest[0:1])
        sc = jnp.zeros((CHUNK, MIX_W), F32)
        for lv in range(n_lv + 1):
            if lv < n_lv:
                e = jnp.exp(-jnp.abs(cs[(lv + 1) * CHUNK:(lv + 2) * CHUNK]))
                ql = (q * e).astype(BF16)
                kl = (k * e).astype(BF16)
            else:
                ql = q.astype(BF16)
                kl = k.astype(BF16)
            sc = sc + msk_ref[0, lv] * _dot_nt(ql, _block_diag_rows(kl, lane_head))
        st = st_scr[...]
        o_ref[0, 0, rows, :] = (_dot(sc.astype(BF16), _block_diag_rows(vb, lane_head))
                                + _dot_nt(q_in, st.astype(BF16)))
        st_scr[...] = st * s_decay + hm_ref[...] * _dot_tn(vb, k_out)


def _hgrn2(p_hg, lb_l):
    batch, rows, _ = p_hg.shape
    nt = rows // TILE
    ops_d, masks_d = _hgrn2_consts()
    n_ops = ops_d.shape[1]
    n_msk = masks_d.shape[1]
    col = lambda c: pl.BlockSpec((1, TILE, MIX_W), lambda b, d, j: (b, _tile_of(d, j, nt), c))
    return pl.pallas_call(
        _hgrn2_kernel,
        grid=(batch, 2, nt),
        in_specs=[col(0),
                  pl.BlockSpec((1, TILE, MIX_W), lambda b, d, j: (b, _tile_of(d, j, nt), 1 + d)),
                  col(3),
                  pl.BlockSpec((1, 1, MIX_W), lambda b, d, j: (d, 0, 0)),
                  pl.BlockSpec((1, n_ops, 2 * CHUNK), lambda b, d, j: (d, 0, 0)),
                  pl.BlockSpec((1, n_msk, CHUNK, MIX_W), lambda b, d, j: (d, 0, 0, 0)),
                  pl.BlockSpec((MIX_W, MIX_W), lambda b, d, j: (0, 0))],
        out_specs=pl.BlockSpec((1, 1, TILE, MIX_W), lambda b, d, j: (d, b, _tile_of(d, j, nt), 0)),
        out_shape=jax.ShapeDtypeStruct((2, batch, rows, MIX_W), F32),
        scratch_shapes=[pltpu.VMEM((MIX_W, MIX_W), F32),
                        pltpu.VMEM((TILE, MIX_W), F32),
                        pltpu.VMEM((TILE, MIX_W), F32)],
        compiler_params=_cparams(3, VMEM_LIMIT),
        name="hgrn2",
    )(p_hg, p_hg, p_hg, lb_l.reshape(2, 1, MIX_W), jnp.asarray(ops_d, dtype=BF16), jnp.asarray(masks_d),
      jnp.asarray(_head_block_mask()))


@functools.lru_cache(maxsize=None)
def _mlstm_consts():
    c = CHUNK
    tri = np.tril(np.ones((c, c)))
    flip = np.eye(c)[::-1]
    tile = lambda mk: np.tile(mk, (1, N_HEADS))
    ops, masks = [], []
    for t in (tri, flip @ tri @ flip):
        ops.append(np.concatenate([t, t], axis=1))
        masks.append(np.stack([tile(t.T), tile(t), tile(np.eye(c))]))
    return np.stack(ops).astype(np.float32), np.stack(masks).astype(np.float32)


def _mlstm_kernel(q_ref, k_ref, v_ref, gi_ref, gf_ref, bi_ref, bf_ref, op_ref, cm_ref, hm_ref, h_ref,
                  ct_scr, n_scr, m_scr, i_scr, f_scr):
    d = pl.program_id(1)
    j = pl.program_id(2)

    @pl.when(j == 0)
    def _():
        ct_scr[...] = jnp.zeros_like(ct_scr)
        n_scr[...] = jnp.zeros_like(n_scr)
        m_scr[...] = jnp.zeros_like(m_scr)

    i_scr[...] = gi_ref[0] + bi_ref[0]
    f_scr[...] = _log_sigmoid(gf_ref[0] + bf_ref[0])
    lane_head = lax.broadcasted_iota(I32, (CHUNK, MIX_W), 1) // HEAD_DIM
    neg_inf = F32(-jnp.inf)
    hm = hm_ref[...]
    hm_b = hm.astype(BF16)

    for i in range(CHUNKS_PER_TILE):
        off = pl.multiple_of(_chunk_of(d, i) * CHUNK, CHUNK)
        rows = pl.ds(off, CHUNK)
        ie = i_scr[rows, :]
        fe = f_scr[rows, :]
        bc = _dot(op_ref[0], _split_hi_lo(fe))
        row = jnp.sum(ie * cm_ref[0, 2] - fe * cm_ref[0, 0], axis=0, keepdims=True)
        tot = jnp.sum(fe, axis=0, keepdims=True)
        m_st = m_scr[...]
        dm = jnp.where(cm_ref[0, 1] > 0.5, bc + row, neg_inf)
        m_in = jnp.zeros((CHUNK, MIX_W), F32)
        for h in range(N_HEADS):
            mh = jnp.max(dm[:, h * HEAD_DIM:(h + 1) * HEAD_DIM], axis=-1, keepdims=True)
            m_in = jnp.where(lane_head == h, mh, m_in)
        from_prev = bc + m_st
        m_t = jnp.maximum(from_prev, m_in)
        q = q_ref[0, rows, :]
        k = k_ref[0, rows, :] * (HEAD_DIM ** -0.5)
        qb = q.astype(BF16)
        vb = v_ref[0, rows, :].astype(BF16)
        w = jnp.exp(dm - m_t) * _dot_nt(qb, _block_diag_rows(k.astype(BF16), lane_head))
        a_prev = jnp.exp(from_prev - m_t)
        ct = ct_scr[...]
        n_st = n_scr[...]
        num = _dot(w.astype(BF16), _block_diag_rows(vb, lane_head)) + a_prev * _dot_nt(qb, ct.astype(BF16))
        den = _dot((w + a_prev * (q * n_st)).astype(BF16), hm_b)
        h_ref[0, 0, rows, :] = num / jnp.maximum(jnp.abs(den), jnp.exp(-m_t))
        e_col = tot - bc + ie
        m_new = jnp.maximum(tot + m_st, jnp.max(e_col, axis=0, keepdims=True))
        a_end = jnp.exp(tot + m_st - m_new)
        kw = k * jnp.exp(e_col - m_new)
        ct_scr[...] = ct * a_end + hm * _dot_tn(vb, kw.astype(BF16))
        n_scr[...] = a_end * n_st + jnp.sum(kw, axis=0, keepdims=True)
        m_scr[...] = m_new


def _mlstm(p_ml, p_gt, bias_i, bias_f):
    batch, rows, _ = p_ml.shape
    nt = rows // TILE
    ops_d, masks_d = _mlstm_consts()
    col = lambda c: pl.BlockSpec((1, TILE, MIX_W), lambda b, d, j: (b, _tile_of(d, j, nt), c))
    gate = lambda kind: pl.BlockSpec((1, TILE, MIX_W), lambda b, d, j: (b, _tile_of(d, j, nt), 2 * d + kind))
    by_dir = lambda shp: pl.BlockSpec((1,) + shp, lambda b, d, j: (d,) + (0,) * len(shp))
    return pl.pallas_call(
        _mlstm_kernel,
        grid=(batch, 2, nt),
        in_specs=[col(0), col(1), col(2), gate(0), gate(1),
                  by_dir((1, MIX_W)), by_dir((1, MIX_W)),
                  by_dir((CHUNK, 2 * CHUNK)), by_dir((3, CHUNK, MIX_W)),
                  pl.BlockSpec((MIX_W, MIX_W), lambda b, d, j: (0, 0))],
        out_specs=pl.BlockSpec((1, 1, TILE, MIX_W), lambda b, d, j: (d, b, _tile_of(d, j, nt), 0)),
        out_shape=jax.ShapeDtypeStruct((2, batch, rows, MIX_W), F32),
        scratch_shapes=[pltpu.VMEM((MIX_W, MIX_W), F32),
                        pltpu.VMEM((1, MIX_W), F32),
                        pltpu.VMEM((1, MIX_W), F32),
                        pltpu.VMEM((TILE, MIX_W), F32),
                        pltpu.VMEM((TILE, MIX_W), F32)],
        compiler_params=_cparams(3, VMEM_LIMIT),
        name="mlstm",
    )(p_ml, p_ml, p_ml, p_gt, p_gt, bias_i, bias_f, jnp.asarray(ops_d, dtype=BF16), jnp.asarray(masks_d),
      jnp.asarray(_head_block_mask()))


def _rope_tables(rows, ctx_len):
    n = rows - ctx_len
    r = jnp.arange(n) // GRID_W
    cc = jnp.arange(n) % GRID_W
    axis_dims = AT_HEAD_DIM // 2
    inv_freq = ROPE_THETA ** (-jnp.arange(0, axis_dims, 2, dtype=F32) / axis_dims)
    ang_r = r.astype(F32)[:, None] * inv_freq
    ang_c = cc.astype(F32)[:, None] * inv_freq
    cos = jnp.concatenate([jnp.cos(ang_r), jnp.cos(ang_r), jnp.cos(ang_c), jnp.cos(ang_c)], -1)
    sin = jnp.concatenate([-jnp.sin(ang_r), jnp.sin(ang_r), -jnp.sin(ang_c), jnp.sin(ang_c)], -1)
    cos = jnp.concatenate([jnp.ones((ctx_len, AT_HEAD_DIM), F32), cos], 0)
    sin = jnp.concatenate([jnp.zeros((ctx_len, AT_HEAD_DIM), F32), sin], 0)
    return cos, sin


def _attn_prep_kernel(p_ref, cos_ref, sin_ref, qg_ref, kg_ref, q_ref, k_ref, v_ref):
    cos = cos_ref[...]
    sin = sin_ref[...]
    lane = lax.broadcasted_iota(I32, (TILE, AT_HEAD_DIM), 1)
    first_half = (lane % (AT_HEAD_DIM // 2)) < (AT_HEAD_DIM // 4)
    quarter = AT_HEAD_DIM // 4

    def norm_rope(x, g):
        y = _rms(x) * g
        partner = jnp.where(first_half, pltpu.roll(y, AT_HEAD_DIM - quarter, axis=1), pltpu.roll(y, quarter, axis=1))
        return y * cos + partner * sin

    scale = AT_HEAD_DIM ** -0.5
    for h in range(AT_HEADS):
        sl = slice(h * AT_HEAD_DIM, (h + 1) * AT_HEAD_DIM)
        q_ref[0, :, sl] = (norm_rope(p_ref[0, :, sl], qg_ref[...]) * scale).astype(BF16)
    for h in range(AT_KV_HEADS):
        sl = slice(h * AT_HEAD_DIM, (h + 1) * AT_HEAD_DIM)
        src = slice((AT_HEADS + h) * AT_HEAD_DIM, (AT_HEADS + h + 1) * AT_HEAD_DIM)
        k_ref[0, :, sl] = norm_rope(p_ref[0, :, src], kg_ref[...]).astype(BF16)
    for h in range(AT_KV_HEADS):
        src = slice((AT_HEADS + AT_KV_HEADS + h) * AT_HEAD_DIM, (AT_HEADS + AT_KV_HEADS + h + 1) * AT_HEAD_DIM)
        v_ref[0, :, 2 * h * AT_HEAD_DIM:(2 * h + 1) * AT_HEAD_DIM] = p_ref[0, :, src].astype(BF16)
        v_ref[0, :, (2 * h + 1) * AT_HEAD_DIM:(2 * h + 2) * AT_HEAD_DIM] = jnp.ones((TILE, AT_HEAD_DIM), BF16)


def _attn_prep(p_at, cos, sin, q_g, k_g):
    batch, rows, _ = p_at.shape
    nt = rows // TILE
    tok = lambda w: pl.BlockSpec((1, TILE, w), lambda b, j: (b, j, 0))
    tab = pl.BlockSpec((TILE, AT_HEAD_DIM), lambda b, j: (j, 0))
    vec = pl.BlockSpec((1, AT_HEAD_DIM), lambda b, j: (0, 0))
    kvw = AT_KV_HEADS * AT_HEAD_DIM
    return pl.pallas_call(
        _attn_prep_kernel,
        grid=(batch, nt),
        in_specs=[tok(AT_IN), tab, tab, vec, vec],
        out_specs=[tok(AT_HEADS * AT_HEAD_DIM), tok(kvw), tok(2 * kvw)],
        out_shape=[jax.ShapeDtypeStruct((batch, rows, AT_HEADS * AT_HEAD_DIM), BF16),
                   jax.ShapeDtypeStruct((batch, rows, kvw), BF16),
                   jax.ShapeDtypeStruct((batch, rows, 2 * kvw), BF16)],
        compiler_params=_cparams(2, VMEM_LIMIT),
        name="attn_prep",
    )(p_at, cos, sin, q_g, k_g)


def _attn_kernel(q_ref, k_ref, v_ref, o_ref):
    j = pl.program_id(2)

    def attend(n_keys):
        q2 = jnp.concatenate([q_ref[0, :, g * AT_HEAD_DIM:(g + 1) * AT_HEAD_DIM] for g in range(AT_GROUP)], axis=0)
        m = jnp.full((AT_GROUP * TILE, 1), -jnp.inf, F32)
        acc = jnp.zeros((AT_GROUP * TILE, 2 * AT_HEAD_DIM), F32)
        bounds = list(range(0, n_keys, ATT_KEY_CHUNK))
        bounds = bounds[:max(len(bounds) - 1, 1)] if n_keys % ATT_KEY_CHUNK else bounds
        bounds.append(n_keys)
        for c0, c1 in zip(bounds[:-1], bounds[1:]):
            s = _dot_nt(q2, k_ref[0, c0:c1, :])
            m_new = jnp.maximum(m, jnp.max(s, axis=-1, keepdims=True))
            p = jnp.exp((s - m_new).astype(BF16))
            acc = acc * jnp.exp(m - m_new) + _dot(p, v_ref[0, c0:c1, :])
            m = m_new
        o = acc[:, 0:AT_HEAD_DIM] / acc[:, AT_HEAD_DIM:AT_HEAD_DIM + 1]
        for g in range(AT_GROUP):
            o_ref[0, :, g * AT_HEAD_DIM:(g + 1) * AT_HEAD_DIM] = o[g * TILE:(g + 1) * TILE]

    @pl.when(j == 0)
    def _():
        attend(TILE)

    @pl.when(j > 0)
    def _():
        attend(k_ref.shape[1])


def _attention(qn, kn, vb):
    batch, rows, _ = qn.shape
    nt = rows // TILE
    gw = AT_GROUP * AT_HEAD_DIM
    return pl.pallas_call(
        _attn_kernel,
        grid=(batch, AT_KV_HEADS, nt),
        in_specs=[pl.BlockSpec((1, TILE, gw), lambda b, h, j: (b, j, h)),
                  pl.BlockSpec((1, rows, AT_HEAD_DIM), lambda b, h, j: (b, 0, h)),
                  pl.BlockSpec((1, rows, 2 * AT_HEAD_DIM), lambda b, h, j: (b, 0, h))],
        out_specs=pl.BlockSpec((1, TILE, gw), lambda b, h, j: (b, j, h)),
        out_shape=jax.ShapeDtypeStruct((batch, rows, AT_HEADS * AT_HEAD_DIM), F32),
        compiler_params=_cparams(3, VMEM_LIMIT),
        name="attention",
    )(qn, kn, vb)


def _outproj_kernel(x_ref, mod_ref, ohf_ref, ohb_ref, gh_ref, omf_ref, omb_ref, gm_ref, at_ref,
                    hgg_ref, mlg_ref, ones_ref, wo_ref, g2_ref, wr_ref, br_ref, low_ref, up_ref,
                    xo_ref, st_ref, pos_ref, gate_ref, tab_ref, cnt_ref, cnt_scr):
    b = pl.program_id(0)
    j = pl.program_id(1)

    @pl.when((b == 0) & (j == 0))
    def _():
        cnt_scr[...] = jnp.zeros_like(cnt_scr)

    mod = mod_ref[0]
    gate1 = mod[:, 2 * D_MODEL:3 * D_MODEL]
    sh2 = mod[:, 3 * D_MODEL:4 * D_MODEL]
    sc2 = mod[:, 4 * D_MODEL:5 * D_MODEL]

    def head_norm(o, g):
        ms = _dot((o * o).astype(BF16), ones_ref[...]) * (1.0 / HEAD_DIM)
        return o * lax.rsqrt(ms + NORM_EPS) * g

    gh = gh_ref[0]
    y_hg = head_norm(ohf_ref[0, 0] + ohb_ref[0, 0], hgg_ref[...]) * (gh / (1.0 + jnp.exp(-gh)))
    y_ml = head_norm(omf_ref[0, 0] + omb_ref[0, 0], mlg_ref[...]) / (1.0 + jnp.exp(-gm_ref[0]))
    proj = (_dot(y_hg.astype(BF16), wo_ref[0:MIX_W, :])
            + _dot(y_ml.astype(BF16), wo_ref[MIX_W:2 * MIX_W, :])
            + _dot(at_ref[0].astype(BF16), wo_ref[2 * MIX_W:, :]))
    x = x_ref[0] + gate1 * proj
    xo_ref[0] = x
    h2 = _rms(x) * g2_ref[...]
    h2 = h2 * (1.0 + sc2) + sh2

    lane = lax.broadcasted_iota(I32, (TILE, LANES), 1)
    lane_f = lane.astype(F32)
    h_hi = h2.astype(BF16)
    h_lo = (h2 - h_hi.astype(F32)).astype(BF16)
    logits = _dot(jnp.concatenate([h_hi, h_lo, h_hi], axis=1), wr_ref[...]) + br_ref[...]
    lg = jnp.where(lane < N_EXPERTS, logits, -jnp.inf)
    multi = jnp.zeros((TILE, LANES), F32)
    hots, vals = [], []
    for _ in range(TOP_K):
        mx = jnp.max(lg, axis=-1, keepdims=True)
        idx = jnp.min(jnp.where(lg == mx, lane_f, F32(LANES)), axis=-1, keepdims=True)
        hot = lane_f == idx
        lg = jnp.where(hot, -jnp.inf, lg)
        multi = multi + hot.astype(F32)
        hots.append(hot)
        vals.append(mx)
    es = [jnp.exp(v - vals[0]) for v in vals]
    den = es[0] + es[1] + es[2] + es[3]

    tile_cnt = jnp.sum(multi, axis=0, keepdims=True)
    tile_start = _dot(jnp.broadcast_to(tile_cnt, (8, LANES)).astype(BF16), up_ref[...])[0:1]
    where_in_tile = _dot(low_ref[...], multi.astype(BF16)) + tile_start
    slot_lane = lax.broadcasted_iota(I32, (TILE, TOP_K * TILE), 1).astype(F32)
    onehot = jnp.zeros((TILE, TOP_K * TILE), F32)
    pos_out = jnp.zeros((TILE, LANES), I32)
    gate_out = jnp.zeros((TILE, LANES), F32)
    for kk in range(TOP_K):
        pos = jnp.sum(jnp.where(hots[kk], where_in_tile, 0.0), axis=-1, keepdims=True)
        onehot = onehot + jnp.where(slot_lane == pos, 1.0, 0.0)
        pos_out = pos_out + jnp.where(lane == kk, pos.astype(I32), 0)
        gate_out = gate_out + jnp.where(lane == kk, es[kk] / den, 0.0)
    sorted_rows = _dot_tn(onehot.astype(BF16), h_hi)
    st_ref[...] = sorted_rows.astype(BF16).reshape(TOP_K * TILE, ROW_SUB, LANES)
    pos_ref[0] = pos_out[:, 0:2 * TOP_K]
    gate_ref[0] = gate_out
    tab_ref[0] = jnp.concatenate([tile_cnt, tile_start, cnt_scr[...], jnp.zeros((5, LANES), F32)], axis=0).astype(I32)
    cnt_scr[...] = cnt_scr[...] + tile_cnt
    cnt_ref[...] = cnt_scr[...]


def _outproj(xs, mod_l, o_hg, p_hg, o_ml, p_ml, att, hg_g, ml_g, w_out_b, g2, w_router_p, b_router_p):
    batch, rows, _ = xs.shape
    nt = rows // TILE
    tok = lambda w, c=0: pl.BlockSpec((1, TILE, w), lambda b, j: (b, j, c))
    dirs = lambda d: pl.BlockSpec((1, 1, TILE, MIX_W), lambda b, j: (d, b, j, 0))
    full = lambda r, c: pl.BlockSpec((r, c), lambda b, j: (0, 0))
    block_ones = np.kron(np.eye(N_HEADS), np.ones((HEAD_DIM, HEAD_DIM))).astype(np.float32)
    strict_low = np.tril(np.ones((TILE, TILE)), -1).astype(np.float32)
    strict_up = np.triu(np.ones((LANES, LANES)), 1).astype(np.float32)
    n_tiles = batch * nt
    return pl.pallas_call(
        _outproj_kernel,
        grid=(batch, nt),
        in_specs=[tok(D_MODEL), _mod_spec(batch),
                  dirs(0), dirs(1), tok(MIX_W, 4),
                  dirs(0), dirs(1), tok(MIX_W, 3),
                  tok(AT_HEADS * AT_HEAD_DIM),
                  full(1, MIX_W), full(1, MIX_W), full(MIX_W, MIX_W),
                  full(D_MODEL, D_MODEL), full(1, D_MODEL),
                  full(3 * D_MODEL, LANES), full(1, LANES), full(TILE, TILE), full(LANES, LANES)],
        out_specs=[tok(D_MODEL),
                   pl.BlockSpec((TOP_K * TILE, ROW_SUB, LANES), lambda b, j: (b * nt + j, 0, 0)),
                   tok(2 * TOP_K), tok(LANES),
                   pl.BlockSpec((1, 8, LANES), lambda b, j: (b * nt + j, 0, 0)),
                   pl.BlockSpec((1, LANES), lambda b, j: (0, 0))],
        out_shape=[jax.ShapeDtypeStruct((batch, rows, D_MODEL), F32),
                   jax.ShapeDtypeStruct((n_tiles * TOP_K * TILE, ROW_SUB, LANES), BF16),
                   jax.ShapeDtypeStruct((batch, rows, 2 * TOP_K), I32),
                   jax.ShapeDtypeStruct((batch, rows, LANES), F32),
                   jax.ShapeDtypeStruct((n_tiles, 8, LANES), I32),
                   jax.ShapeDtypeStruct((1, LANES), F32)],
        scratch_shapes=[pltpu.VMEM((1, LANES), F32)],
        compiler_params=_cparams(2, VMEM_LIMIT),
        name="outproj_router",
    )(xs, mod_l, o_hg, o_hg, p_hg, o_ml, o_ml, p_ml, att, hg_g, ml_g, jnp.asarray(block_ones, dtype=BF16),
      w_out_b, g2, w_router_p, b_router_p, jnp.asarray(strict_low, dtype=BF16),
      jnp.asarray(strict_up, dtype=BF16))


_RUN_BITS = tuple(1 << s for s in range(TILE.bit_length() - 1, -1, -1))


def _copy_run(src, dst, sem, n, src0, dst0, wait):
    for bit in _RUN_BITS:
        done = n - (n & (2 * bit - 1))

        @pl.when((n & bit) != 0)
        def _():
            cp = pltpu.make_async_copy(src.at[pl.ds(src0 + done, bit)], dst.at[pl.ds(dst0 + done, bit)], sem)
            cp.wait() if wait else cp.start()


def _move_kernel(n_ref, a_ref, b_ref, zn_ref, zo_ref, src_ref, zero_ref, dst_ref, sem, *, fill):
    t = pl.program_id(0)
    for wait in (False, True):
        def per_expert(e, carry):
            i = t * N_EXPERTS + e
            _copy_run(src_ref, dst_ref, sem, n_ref[i], a_ref[i], b_ref[i], wait)
            return carry

        lax.fori_loop(0, N_EXPERTS, per_expert, 0)

        if fill:
            @pl.when(t == 0)
            def _():
                def per_gap(e, carry):
                    def per_piece(i, c):
                        left = jnp.minimum(zn_ref[e] - i * TILE, TILE)
                        _copy_run(zero_ref, dst_ref, sem, left, 0, zo_ref[e] + i * TILE, wait)
                        return c

                    lax.fori_loop(0, (zn_ref[e] + TILE - 1) // TILE, per_piece, 0)
                    return carry

                lax.fori_loop(0, zn_ref.shape[0], per_gap, 0)


def _move_rows(run_n, run_a, run_b, src, n_dst_rows, fill=None):
    n_tiles = run_n.shape[0] // N_EXPERTS
    zn, zo = fill if fill is not None else (jnp.zeros((1,), I32), jnp.zeros((1,), I32))
    zeros = jnp.zeros((TILE,) + src.shape[1:], src.dtype)
    grid_spec = pltpu.PrefetchScalarGridSpec(
        num_scalar_prefetch=5,
        grid=(n_tiles,),
        in_specs=[pl.BlockSpec(memory_space=pl.ANY), pl.BlockSpec(memory_space=pl.ANY)],
        out_specs=pl.BlockSpec(memory_space=pl.ANY),
        scratch_shapes=[pltpu.SemaphoreType.DMA(())],
    )
    return pl.pallas_call(
        functools.partial(_move_kernel, fill=fill is not None),
        grid_spec=grid_spec,
        out_shape=jax.ShapeDtypeStruct((n_dst_rows,) + src.shape[1:], src.dtype),
        compiler_params=_cparams(1, VMEM_LIMIT),
        name="moe_move_rows",
    )(run_n, run_a, run_b, zn, zo, src, zeros)


def _expert_kernel(be_ref, nv_ref, x_ref, wgu_ref, bgu_ref, wd_ref, bd_ref, y_ref, wgu_scr, wd_scr):
    i = pl.program_id(0)

    @pl.when(i < nv_ref[0])
    def _():
        prev = be_ref[jnp.maximum(i - 1, 0)]

        @pl.when((i == 0) | (prev != be_ref[i]))
        def _():
            wgu_scr[...] = wgu_ref[0, 0].astype(BF16)
            wd_scr[...] = wd_ref[0, 0].astype(BF16)

        x = x_ref[...].reshape(MOE_BLOCK, D_MODEL)
        gu = _dot(x, wgu_scr[...]) + bgu_ref[0, 0]
        gate = jnp.minimum(gu[:, 0:D_EXPERT], SWIGLU_LIMIT)
        up = jnp.clip(gu[:, D_EXPERT:], -SWIGLU_LIMIT, SWIGLU_LIMIT)
        act = (up + 1.0) * gate * (1.0 / (1.0 + jnp.exp(-SWIGLU_ALPHA * gate)))
        y = _dot(act.astype(BF16), wd_scr[...]) + bd_ref[0, 0]
        y_ref[...] = y.astype(BF16).reshape(MOE_BLOCK, ROW_SUB, LANES)

    @pl.when(i >= nv_ref[0])
    def _():
        y_ref[...] = jnp.zeros_like(y_ref)


def _experts(layer, block_e, n_valid, xs_sorted, w_gu, b_gu, w_down, b_down):
    n_slots = xs_sorted.shape[0]
    nb = n_slots // MOE_BLOCK
    depth = w_gu.shape[0]
    clamp = lambda i, nv: jnp.minimum(i, jnp.maximum(nv[0] - 1, 0))
    grid_spec = pltpu.PrefetchScalarGridSpec(
        num_scalar_prefetch=2,
        grid=(nb,),
        in_specs=[pl.BlockSpec((MOE_BLOCK, ROW_SUB, LANES), lambda i, be, nv: (clamp(i, nv), 0, 0)),
                  pl.BlockSpec((1, 1, D_MODEL, 2 * D_EXPERT), lambda i, be, nv: (layer, be[i], 0, 0)),
                  pl.BlockSpec((1, 1, 1, 2 * D_EXPERT), lambda i, be, nv: (layer, be[i], 0, 0)),
                  pl.BlockSpec((1, 1, D_EXPERT, D_MODEL), lambda i, be, nv: (layer, be[i], 0, 0)),
                  pl.BlockSpec((1, 1, 1, D_MODEL), lambda i, be, nv: (layer, be[i], 0, 0))],
        out_specs=pl.BlockSpec((MOE_BLOCK, ROW_SUB, LANES), lambda i, be, nv: (i, 0, 0)),
        scratch_shapes=[pltpu.VMEM((D_MODEL, 2 * D_EXPERT), BF16),
                        pltpu.VMEM((D_EXPERT, D_MODEL), BF16)],
    )
    return pl.pallas_call(
        _expert_kernel,
        grid_spec=grid_spec,
        out_shape=jax.ShapeDtypeStruct((n_slots, ROW_SUB, LANES), BF16),
        compiler_params=_cparams(1, VMEM_LIMIT),
        name="moe_experts",
    )(block_e, n_valid, xs_sorted, w_gu, b_gu.reshape(depth, N_EXPERTS, 1, 2 * D_EXPERT),
      w_down, b_down.reshape(depth, N_EXPERTS, 1, D_MODEL))


def _combine_kernel(x_ref, mod_ref, gate_ref, pos_ref, y_ref, o_ref):
    gates = gate_ref[0]
    pos = pos_ref[0].astype(F32)
    slot_lane = lax.broadcasted_iota(I32, (TILE, TOP_K * TILE), 1).astype(F32)
    wmat = jnp.zeros((TILE, TOP_K * TILE), F32)
    for kk in range(TOP_K):
        wmat = wmat + jnp.where(slot_lane == pos[:, kk:kk + 1], gates[:, kk:kk + 1], 0.0)
    f = _dot(wmat.astype(BF16), y_ref[...].reshape(TOP_K * TILE, D_MODEL))
    gate2 = mod_ref[0][:, 5 * D_MODEL:6 * D_MODEL]
    o_ref[0] = x_ref[0] + gate2 * f


def _combine(xs, mod_l, gates, pos, y_tiles):
    batch, rows, _ = xs.shape
    nt = rows // TILE
    tok = lambda w: pl.BlockSpec((1, TILE, w), lambda b, j: (b, j, 0))
    return pl.pallas_call(
        _combine_kernel,
        grid=(batch, nt),
        in_specs=[tok(D_MODEL), _mod_spec(batch), tok(LANES), tok(2 * TOP_K),
                  pl.BlockSpec((TOP_K * TILE, ROW_SUB, LANES), lambda b, j: (b * nt + j, 0, 0))],
        out_specs=tok(D_MODEL),
        out_shape=jax.ShapeDtypeStruct((batch, rows, D_MODEL), F32),
        compiler_params=_cparams(2, VMEM_LIMIT),
        name="moe_combine",
    )(xs, mod_l, gates, pos, y_tiles)


def _moe_plan(counts, tables, n_tokens):
    cnt = counts[0, :N_EXPERTS].astype(I32)
    padded = (cnt + MOE_BLOCK - 1) // MOE_BLOCK * MOE_BLOCK
    pend = jnp.cumsum(padded)
    pstart = pend - padded
    nb = -(-(n_tokens * TOP_K) // MOE_BLOCK) + N_EXPERTS
    block_row = jnp.arange(nb, dtype=I32) * MOE_BLOCK
    block_e = jnp.minimum(jnp.sum((pend[None, :] <= block_row[:, None]).astype(I32), axis=1), N_EXPERTS - 1)
    n_valid = (pend[-1] // MOE_BLOCK).astype(I32).reshape(1)
    n_tiles = tables.shape[0]
    run_n = tables[:, 0, :N_EXPERTS]
    tile_pos = tables[:, 1, :N_EXPERTS] + jnp.arange(n_tiles, dtype=I32)[:, None] * (TOP_K * TILE)
    slot_pos = tables[:, 2, :N_EXPERTS] + pstart[None, :]
    fill_n = jnp.concatenate([padded - cnt, (nb * MOE_BLOCK - pend[-1])[None]])
    fill_at = jnp.concatenate([pstart + cnt, pend[-1:]])
    flat = lambda a: a.reshape(-1).astype(I32)
    return (flat(run_n), flat(tile_pos), flat(slot_pos), (flat(fill_n), flat(fill_at)),
            block_e.astype(I32), n_valid, nb * MOE_BLOCK)


def _arrange_w_in(w_in_l):
    hg = w_in_l[:, 0:HG_IN]
    ml0 = HG_IN
    ml = w_in_l[:, ml0:ml0 + ML_IN]
    gt = w_in_l[:, ml0 + ML_IN:ml0 + ML_IN + 4 * N_HEADS]
    at = w_in_l[:, ml0 + ML_IN + 4 * N_HEADS:]
    gt = jnp.repeat(_gate_order(gt.T).T, HEAD_DIM, axis=1)
    return jnp.concatenate([hg, ml, at, gt], 1).astype(BF16)


def _gate_order(g):
    g = g.reshape((4, N_HEADS) + g.shape[1:]) if g.shape[0] == 4 * N_HEADS else g
    return jnp.concatenate([g[0], g[2], g[1], g[3]], axis=0)


def _hgrn2_lower_bounds(lb_logits):
    p = jax.nn.softmax(lb_logits.astype(F32), axis=1)
    cum = jnp.cumsum(p, axis=1)
    return cum - cum[:, :1]


def kernel(x, c, ctx, c_ctx, w_mod, b_mod, norm1_g, norm2_g, w_in, w_out, hg_lb_logits, hg_norm_g,
           ml_gate_bias, ml_norm_g, q_norm_g, k_norm_g, w_router, b_router, w_gu, b_gu, w_down, b_down):
    batch, seq, d = x.shape
    ctx_len = ctx.shape[1]
    depth = w_mod.shape[0]
    assert d == D_MODEL and ctx_len == TILE and seq % TILE == 0 and batch + 1 <= MOD_ROWS
    rows = ctx_len + seq
    n_tokens = batch * rows

    cvec = jnp.concatenate([c, c_ctx[None, :], jnp.zeros((MOD_ROWS - batch - 1, d), F32)], 0)
    mod = _modulation(cvec, w_mod, b_mod).reshape(depth, MOD_ROWS, 1, N_MOD)
    hg_lb = _hgrn2_lower_bounds(hg_lb_logits)
    cos, sin = _rope_tables(rows, ctx_len)
    xs = jnp.concatenate([ctx, x], axis=1)

    for layer in range(depth):
        mod_l = mod[layer]
        p_hg, p_ml, p_at, p_gt = _inproj(xs, mod_l, norm1_g[layer][None, :], _arrange_w_in(w_in[layer]))
        o_hg = _hgrn2(p_hg, hg_lb[:, layer])
        gb = jnp.repeat(_gate_order(ml_gate_bias[layer]), HEAD_DIM).reshape(4, 1, MIX_W)
        o_ml = _mlstm(p_ml, p_gt, gb[0::2], gb[1::2])
        qn, kn, vb = _attn_prep(p_at, cos, sin, q_norm_g[layer][None, :], k_norm_g[layer][None, :])
        att = _attention(qn, kn, vb)
        w_router_p = jnp.pad(w_router[layer], ((0, 0), (0, LANES - N_EXPERTS)))
        wr_hi = w_router_p.astype(BF16)
        wr_lo = (w_router_p - wr_hi.astype(F32)).astype(BF16)
        w_router_p = jnp.concatenate([wr_hi, wr_hi, wr_lo], axis=0)
        b_router_p = jnp.pad(b_router[layer], (0, LANES - N_EXPERTS))[None, :]
        xs, h_tiles, pos, gates, tables, counts = _outproj(
            xs, mod_l, o_hg, p_hg, o_ml, p_ml, att,
            jnp.tile(hg_norm_g[layer], N_HEADS)[None, :], jnp.tile(ml_norm_g[layer], N_HEADS)[None, :],
            w_out[layer].astype(BF16), norm2_g[layer][None, :], w_router_p, b_router_p)
        run_n, tile_pos, slot_pos, fill, block_e, n_valid, n_slots = _moe_plan(counts, tables, n_tokens)
        xs_sorted = _move_rows(run_n, tile_pos, slot_pos, h_tiles, n_slots, fill=fill)
        y_sorted = _experts(layer, block_e, n_valid, xs_sorted, w_gu, b_gu, w_down, b_down)
        y_tiles = _move_rows(run_n, slot_pos, tile_pos, y_sorted, n_tokens * TOP_K)
        xs = _combine(xs, mod_l, gates, pos, y_tiles)
    return xs[:, ctx_len:, :]
```

```python
import functools

import numpy as np
import jax
import jax.numpy as jnp
from jax import lax
from jax.experimental import pallas as pl
from jax.experimental.pallas import tpu as pltpu

F32 = jnp.float32
BF16 = jnp.bfloat16
I32 = jnp.int32
HIGHEST = lax.Precision.HIGHEST

D_MODEL = 1024
NORM_EPS = 1e-6
CHUNK = 64
TILE = 256
CHUNKS_PER_TILE = TILE // CHUNK
GRID_W = 64
ROPE_THETA = 10000.0

N_HEADS = 4
HEAD_DIM = 64
MIX_W = N_HEADS * HEAD_DIM
AT_HEADS = 4
AT_KV_HEADS = 2
AT_HEAD_DIM = 128
AT_GROUP = AT_HEADS // AT_KV_HEADS
ATT_KEY_CHUNK = 1024

HG_IN = 5 * MIX_W
ML_IN = 4 * MIX_W
AT_IN = (AT_HEADS + 2 * AT_KV_HEADS) * AT_HEAD_DIM
GATE_W = 4 * MIX_W

N_EXPERTS = 32
TOP_K = 4
D_EXPERT = 1024
SWIGLU_LIMIT = 7.0
SWIGLU_ALPHA = 1.702
MOE_BLOCK = 512
ROW_SUB = D_MODEL // 128
LANES = 128
N_MOD = 6 * D_MODEL
MOD_ROWS = 16

VMEM_LIMIT = 56 * 1024 * 1024


def _cparams(n_axes, vmem=None):
    kw = dict(dimension_semantics=("arbitrary",) * n_axes)
    if vmem is not None:
        kw["vmem_limit_bytes"] = vmem
    return pltpu.CompilerParams(**kw)


def _dot(a, b, precision=None):
    return jnp.dot(a, b, preferred_element_type=F32, precision=precision)


def _dot_nt(a, b):
    return lax.dot_general(a, b, (((1,), (1,)), ((), ())), preferred_element_type=F32)


def _dot_tn(a, b, precision=None):
    return lax.dot_general(a, b, (((0,), (0,)), ((), ())), preferred_element_type=F32,
                           precision=precision)


def _log_sigmoid(z):
    return jnp.minimum(z, 0.0) - jnp.log1p(jnp.exp(-jnp.abs(z)))


def _rms(x, eps=NORM_EPS):
    return x * lax.rsqrt(jnp.mean(x * x, axis=-1, keepdims=True) + eps)


def _mod_kernel(c_ref, w_ref, b_ref, o_ref):
    cv = c_ref[...]
    s = cv * (1.0 / (1.0 + jnp.exp(-cv)))
    o_ref[0] = _dot(s, w_ref[0], precision=HIGHEST) + b_ref[0]


def _modulation(cvec, w_mod, b_mod):
    depth = w_mod.shape[0]
    nblk = 1536
    return pl.pallas_call(
        _mod_kernel,
        grid=(depth, N_MOD // nblk),
        in_specs=[pl.BlockSpec((MOD_ROWS, D_MODEL), lambda l, n: (0, 0)),
                  pl.BlockSpec((1, D_MODEL, nblk), lambda l, n: (l, 0, n)),
                  pl.BlockSpec((1, 1, nblk), lambda l, n: (l, 0, n))],
        out_specs=pl.BlockSpec((1, MOD_ROWS, nblk), lambda l, n: (l, 0, n)),
        out_shape=jax.ShapeDtypeStruct((depth, MOD_ROWS, N_MOD), F32),
        compiler_params=_cparams(2, VMEM_LIMIT),
        name="modulation",
    )(cvec, w_mod, b_mod.reshape(depth, 1, N_MOD))


def _mod_spec(batch):
    return pl.BlockSpec((1, 1, N_MOD), lambda b, j: (jnp.where(j == 0, batch, b), 0, 0))


def _inproj_kernel(x_ref, mod_ref, g_ref, w_ref, hg_ref, ml_ref, at_ref, gt_ref):
    x = x_ref[0]
    mod = mod_ref[0]
    sh = mod[:, 0:D_MODEL]
    sc = mod[:, D_MODEL:2 * D_MODEL]
    h = _rms(x) * g_ref[...]
    h = (h * (1.0 + sc) + sh).astype(BF16)
    o0 = 0
    for ref, width in ((hg_ref, HG_IN), (ml_ref, ML_IN), (at_ref, AT_IN), (gt_ref, GATE_W)):
        ref[0] = _dot(h, w_ref[:, o0:o0 + width])
        o0 += width


def _inproj(xs, mod_l, g1, w_in_r):
    batch, rows, _ = xs.shape
    nt = rows // TILE
    n_all = HG_IN + ML_IN + AT_IN + GATE_W
    tok = lambda w: pl.BlockSpec((1, TILE, w), lambda b, j: (b, j, 0))
    return pl.pallas_call(
        _inproj_kernel,
        grid=(batch, nt),
        in_specs=[tok(D_MODEL), _mod_spec(batch),
                  pl.BlockSpec((1, D_MODEL), lambda b, j: (0, 0)),
                  pl.BlockSpec((D_MODEL, n_all), lambda b, j: (0, 0))],
        out_specs=[tok(HG_IN), tok(ML_IN), tok(AT_IN), tok(GATE_W)],
        out_shape=[jax.ShapeDtypeStruct((batch, rows, w), F32)
                   for w in (HG_IN, ML_IN, AT_IN, GATE_W)],
        compiler_params=_cparams(2, VMEM_LIMIT),
        name="inproj",
    )(xs, mod_l, g1, w_in_r)


def _tile_of(d, j, nt):
    return j + d * jnp.where(j > 0, nt - 2 * j, 0)


def _chunk_of(d, i):
    return i + d * (CHUNKS_PER_TILE - 1 - 2 * i)


_HG_LEVELS = (32, 16, 8, 4, 2, 1)


def _head_block_mask():
    h = np.arange(MIX_W) // HEAD_DIM
    return (h[:, None] == h[None, :]).astype(np.float32)


def _split_hi_lo(x):
    hi = x.astype(BF16)
    lo = (x - hi.astype(F32)).astype(BF16)
    return jnp.concatenate([hi, lo], axis=0)


def _block_diag_rows(x, lane_head):
    zero = jnp.zeros_like(x)
    return jnp.concatenate([jnp.where(lane_head == h, x, zero) for h in range(N_HEADS)], axis=0)


@functools.lru_cache(maxsize=None)
def _hgrn2_consts():
    c = CHUNK
    tri = np.tril(np.ones((c, c)))
    ops = [tri]
    masks = []
    pos = np.arange(c)
    for m in _HG_LEVELS:
        blk = pos // (2 * m)
        sel = np.zeros((c, c))
        sel[pos, 2 * m * blk + m - 1] = 1.0
        ops.append(tri - sel @ tri)
        in_b = (pos % (2 * m)) >= m
        masks.append(((blk[:, None] == blk[None, :]) & in_b[:, None] & ~in_b[None, :]).astype(np.float64))
    masks.append(np.eye(c))
    ops.append(np.ones((c, c)) - tri)
    flip = np.eye(c)[::-1]
    stack = lambda xs: np.concatenate(xs, 0)
    ops_d = np.stack([stack(ops), stack([flip @ o @ flip for o in ops])])
    ops_d = np.concatenate([ops_d, ops_d], axis=2)
    tile = lambda mk: np.tile(mk, (1, N_HEADS))
    masks_d = np.stack([np.stack([tile(mk) for mk in masks]), np.stack([tile(mk[::-1, ::-1]) for mk in masks])])
    return ops_d.astype(np.float32), masks_d.astype(np.float32)


def _hgrn2_kernel(q_ref, z_ref, v_ref, lb_ref, ops_ref, msk_ref, hm_ref, o_ref, st_scr, lf_scr, k_scr):
    d = pl.program_id(1)
    j = pl.program_id(2)

    @pl.when(j == 0)
    def _():
        st_scr[...] = jnp.zeros_like(st_scr)

    lb = lb_ref[0]
    z = z_ref[0]
    a = jnp.log(lb)
    cterm = jnp.log1p(-lb) + _log_sigmoid(z)
    lf_scr[...] = jnp.maximum(a, cterm) + jnp.log1p(jnp.exp(-jnp.abs(a - cterm)))
    k_scr[...] = (1.0 - lb) / (1.0 + jnp.exp(z))
    n_lv = len(_HG_LEVELS)
    lane_head = lax.broadcasted_iota(I32, (CHUNK, MIX_W), 1) // HEAD_DIM

    for i in range(CHUNKS_PER_TILE):
        off = pl.multiple_of(_chunk_of(d, i) * CHUNK, CHUNK)
        rows = pl.ds(off, CHUNK)
        cs = _dot(ops_ref[0], _split_hi_lo(lf_scr[rows, :]))
        b = cs[0:CHUNK]
        rest = cs[(n_lv + 1) * CHUNK:(n_lv + 2) * CHUNK]
        q = q_ref[0, rows, :]
        k = k_scr[rows, :]
        vb = v_ref[0, rows, :].astype(BF16)
        q_in = (q * jnp.exp(b)).astype(BF16)
        k_out = (k * jnp.exp(rest)).astype(BF16)
        s_decay = jnp.exp(b[0:1] + rest[0:1])
        sc = jnp.zeros((CHUNK, MIX_W), F32)
        for lv in range(n_lv + 1):
            if lv < n_lv:
                e = jnp.exp(-jnp.abs(cs[(lv + 1) * CHUNK:(lv + 2) * CHUNK]))
                ql = (q * e).astype(BF16)
                kl = (k * e).astype(BF16)
            else:
                ql = q.astype(BF16)
                kl = k.astype(BF16)
            sc = sc + msk_ref[0, lv] * _dot_nt(ql, _block_diag_rows(kl, lane_head))
        st = st_scr[...]
        o_ref[0, 0, rows, :] = (_dot(sc.astype(BF16), _block_diag_rows(vb, lane_head))
                                + _dot_nt(q_in, st.astype(BF16)))
        st_scr[...] = st * s_decay + hm_ref[...] * _dot_tn(vb, k_out)


def _hgrn2(p_hg, lb_l):
    batch, rows, _ = p_hg.shape
    nt = rows // TILE
    ops_d, masks_d = _hgrn2_consts()
    n_ops = ops_d.shape[1]
    n_msk = masks_d.shape[1]
    col = lambda c: pl.BlockSpec((1, TILE, MIX_W), lambda b, d, j: (b, _tile_of(d, j, nt), c))
    return pl.pallas_call(
        _hgrn2_kernel,
        grid=(batch, 2, nt),
        in_specs=[col(0),
                  pl.BlockSpec((1, TILE, MIX_W), lambda b, d, j: (b, _tile_of(d, j, nt), 1 + d)),
                  col(3),
                  pl.BlockSpec((1, 1, MIX_W), lambda b, d, j: (d, 0, 0)),
                  pl.BlockSpec((1, n_ops, 2 * CHUNK), lambda b, d, j: (d, 0, 0)),
                  pl.BlockSpec((1, n_msk, CHUNK, MIX_W), lambda b, d, j: (d, 0, 0, 0)),
                  pl.BlockSpec((MIX_W, MIX_W), lambda b, d, j: (0, 0))],
        out_specs=pl.BlockSpec((1, 1, TILE, MIX_W), lambda b, d, j: (d, b, _tile_of(d, j, nt), 0)),
        out_shape=jax.ShapeDtypeStruct((2, batch, rows, MIX_W), F32),
        scratch_shapes=[pltpu.VMEM((MIX_W, MIX_W), F32),
                        pltpu.VMEM((TILE, MIX_W), F32),
                        pltpu.VMEM((TILE, MIX_W), F32)],
        compiler_params=_cparams(3, VMEM_LIMIT),
        name="hgrn2",
    )(p_hg, p_hg, p_hg, lb_l.reshape(2, 1, MIX_W), jnp.asarray(ops_d, dtype=BF16), jnp.asarray(masks_d),
      jnp.asarray(_head_block_mask()))


@functools.lru_cache(maxsize=None)
def _mlstm_consts():
    c = CHUNK
    tri = np.tril(np.ones((c, c)))
    flip = np.eye(c)[::-1]
    tile = lambda mk: np.tile(mk, (1, N_HEADS))
    ops, masks = [], []
    for t in (tri, flip @ tri @ flip):
        ops.append(np.concatenate([t, t], axis=1))
        masks.append(np.stack([tile(t.T), tile(t), tile(np.eye(c))]))
    return np.stack(ops).astype(np.float32), np.stack(masks).astype(np.float32)


def _mlstm_kernel(q_ref, k_ref, v_ref, gi_ref, gf_ref, bi_ref, bf_ref, op_ref, cm_ref, hm_ref, h_ref,
                  ct_scr, n_scr, m_scr, i_scr, f_scr):
    d = pl.program_id(1)
    j = pl.program_id(2)

    @pl.when(j == 0)
    def _():
        ct_scr[...] = jnp.zeros_like(ct_scr)
        n_scr[...] = jnp.zeros_like(n_scr)
        m_scr[...] = jnp.zeros_like(m_scr)

    i_scr[...] = gi_ref[0] + bi_ref[0]
    f_scr[...] = _log_sigmoid(gf_ref[0] + bf_ref[0])
    lane_head = lax.broadcasted_iota(I32, (CHUNK, MIX_W), 1) // HEAD_DIM
    neg_inf = F32(-jnp.inf)
    hm = hm_ref[...]
    hm_b = hm.astype(BF16)

    for i in range(CHUNKS_PER_TILE):
        off = pl.multiple_of(_chunk_of(d, i) * CHUNK, CHUNK)
        rows = pl.ds(off, CHUNK)
        ie = i_scr[rows, :]
        fe = f_scr[rows, :]
        bc = _dot(op_ref[0], _split_hi_lo(fe))
        row = jnp.sum(ie * cm_ref[0, 2] - fe * cm_ref[0, 0], axis=0, keepdims=True)
        tot = jnp.sum(fe, axis=0, keepdims=True)
        m_st = m_scr[...]
        dm = jnp.where(cm_ref[0, 1] > 0.5, bc + row, neg_inf)
        m_in = jnp.zeros((CHUNK, MIX_W), F32)
        for h in range(N_HEADS):
            mh = jnp.max(dm[:, h * HEAD_DIM:(h + 1) * HEAD_DIM], axis=-1, keepdims=True)
            m_in = jnp.where(lane_head == h, mh, m_in)
        from_prev = bc + m_st
        m_t = jnp.maximum(from_prev, m_in)
        q = q_ref[0, rows, :]
        k = k_ref[0, rows, :] * (HEAD_DIM ** -0.5)
        qb = q.astype(BF16)
        vb = v_ref[0, rows, :].astype(BF16)
        w = jnp.exp(dm - m_t) * _dot_nt(qb, _block_diag_rows(k.astype(BF16), lane_head))
        a_prev = jnp.exp(from_prev - m_t)
        ct = ct_scr[...]
        n_st = n_scr[...]
        num = _dot(w.astype(BF16), _block_diag_rows(vb, lane_head)) + a_prev * _dot_nt(qb, ct.astype(BF16))
        den = _dot((w + a_prev * (q * n_st)).astype(BF16), hm_b)
        h_ref[0, 0, rows, :] = num / jnp.maximum(jnp.abs(den), jnp.exp(-m_t))
        e_col = tot - bc + ie
        m_new = jnp.maximum(tot + m_st, jnp.max(e_col, axis=0, keepdims=True))
        a_end = jnp.exp(tot + m_st - m_new)
        kw = k * jnp.exp(e_col - m_new)
        ct_scr[...] = ct * a_end + hm * _dot_tn(vb, kw.astype(BF16))
        n_scr[...] = a_end * n_st + jnp.sum(kw, axis=0, keepdims=True)
        m_scr[...] = m_new


def _mlstm(p_ml, p_gt, bias_i, bias_f):
    batch, rows, _ = p_ml.shape
    nt = rows // TILE
    ops_d, masks_d = _mlstm_consts()
    col = lambda c: pl.BlockSpec((1, TILE, MIX_W), lambda b, d, j: (b, _tile_of(d, j, nt), c))
    gate = lambda kind: pl.BlockSpec((1, TILE, MIX_W), lambda b, d, j: (b, _tile_of(d, j, nt), 2 * d + kind))
    by_dir = lambda shp: pl.BlockSpec((1,) + shp, lambda b, d, j: (d,) + (0,) * len(shp))
    return pl.pallas_call(
        _mlstm_kernel,
        grid=(batch, 2, nt),
        in_specs=[col(0), col(1), col(2), gate(0), gate(1),
                  by_dir((1, MIX_W)), by_dir((1, MIX_W)),
                  by_dir((CHUNK, 2 * CHUNK)), by_dir((3, CHUNK, MIX_W)),
                  pl.BlockSpec((MIX_W, MIX_W), lambda b, d, j: (0, 0))],
        out_specs=pl.BlockSpec((1, 1, TILE, MIX_W), lambda b, d, j: (d, b, _tile_of(d, j, nt), 0)),
        out_shape=jax.ShapeDtypeStruct((2, batch, rows, MIX_W), F32),
        scratch_shapes=[pltpu.VMEM((MIX_W, MIX_W), F32),
                        pltpu.VMEM((1, MIX_W), F32),
                        pltpu.VMEM((1, MIX_W), F32),
                        pltpu.VMEM((TILE, MIX_W), F32),
                        pltpu.VMEM((TILE, MIX_W), F32)],
        compiler_params=_cparams(3, VMEM_LIMIT),
        name="mlstm",
    )(p_ml, p_ml, p_ml, p_gt, p_gt, bias_i, bias_f, jnp.asarray(ops_d, dtype=BF16), jnp.asarray(masks_d),
      jnp.asarray(_head_block_mask()))


def _rope_tables(rows, ctx_len):
    n = rows - ctx_len
    r = jnp.arange(n) // GRID_W
    cc = jnp.arange(n) % GRID_W
    axis_dims = AT_HEAD_DIM // 2
    inv_freq = ROPE_THETA ** (-jnp.arange(0, axis_dims, 2, dtype=F32) / axis_dims)
    ang_r = r.astype(F32)[:, None] * inv_freq
    ang_c = cc.astype(F32)[:, None] * inv_freq
    cos = jnp.concatenate([jnp.cos(ang_r), jnp.cos(ang_r), jnp.cos(ang_c), jnp.cos(ang_c)], -1)
    sin = jnp.concatenate([-jnp.sin(ang_r), jnp.sin(ang_r), -jnp.sin(ang_c), jnp.sin(ang_c)], -1)
    cos = jnp.concatenate([jnp.ones((ctx_len, AT_HEAD_DIM), F32), cos], 0)
    sin = jnp.concatenate([jnp.zeros((ctx_len, AT_HEAD_DIM), F32), sin], 0)
    return cos, sin


def _attn_prep_kernel(p_ref, cos_ref, sin_ref, qg_ref, kg_ref, q_ref, k_ref, v_ref):
    cos = cos_ref[...]
    sin = sin_ref[...]
    lane = lax.broadcasted_iota(I32, (TILE, AT_HEAD_DIM), 1)
    first_half = (lane % (AT_HEAD_DIM // 2)) < (AT_HEAD_DIM // 4)
    quarter = AT_HEAD_DIM // 4

    def norm_rope(x, g):
        y = _rms(x) * g
        partner = jnp.where(first_half, pltpu.roll(y, AT_HEAD_DIM - quarter, axis=1), pltpu.roll(y, quarter, axis=1))
        return y * cos + partner * sin

    scale = AT_HEAD_DIM ** -0.5
    for h in range(AT_HEADS):
        sl = slice(h * AT_HEAD_DIM, (h + 1) * AT_HEAD_DIM)
        q_ref[0, :, sl] = (norm_rope(p_ref[0, :, sl], qg_ref[...]) * scale).astype(BF16)
    for h in range(AT_KV_HEADS):
        sl = slice(h * AT_HEAD_DIM, (h + 1) * AT_HEAD_DIM)
        src = slice((AT_HEADS + h) * AT_HEAD_DIM, (AT_HEADS + h + 1) * AT_HEAD_DIM)
        k_ref[0, :, sl] = norm_rope(p_ref[0, :, src], kg_ref[...]).astype(BF16)
    for h in range(AT_KV_HEADS):
        src = slice((AT_HEADS + AT_KV_HEADS + h) * AT_HEAD_DIM, (AT_HEADS + AT_KV_HEADS + h + 1) * AT_HEAD_DIM)
        v_ref[0, :, 2 * h * AT_HEAD_DIM:(2 * h + 1) * AT_HEAD_DIM] = p_ref[0, :, src].astype(BF16)
        v_ref[0, :, (2 * h + 1) * AT_HEAD_DIM:(2 * h + 2) * AT_HEAD_DIM] = jnp.ones((TILE, AT_HEAD_DIM), BF16)


def _attn_prep(p_at, cos, sin, q_g, k_g):
    batch, rows, _ = p_at.shape
    nt = rows // TILE
    tok = lambda w: pl.BlockSpec((1, TILE, w), lambda b, j: (b, j, 0))
    tab = pl.BlockSpec((TILE, AT_HEAD_DIM), lambda b, j: (j, 0))
    vec = pl.BlockSpec((1, AT_HEAD_DIM), lambda b, j: (0, 0))
    kvw = AT_KV_HEADS * AT_HEAD_DIM
    return pl.pallas_call(
        _attn_prep_kernel,
        grid=(batch, nt),
        in_specs=[tok(AT_IN), tab, tab, vec, vec],
        out_specs=[tok(AT_HEADS * AT_HEAD_DIM), tok(kvw), tok(2 * kvw)],
        out_shape=[jax.ShapeDtypeStruct((batch, rows, AT_HEADS * AT_HEAD_DIM), BF16),
                   jax.ShapeDtypeStruct((batch, rows, kvw), BF16),
                   jax.ShapeDtypeStruct((batch, rows, 2 * kvw), BF16)],
        compiler_params=_cparams(2, VMEM_LIMIT),
        name="attn_prep",
    )(p_at, cos, sin, q_g, k_g)


def _attn_kernel(q_ref, k_ref, v_ref, o_ref):
    j = pl.program_id(2)

    def attend(n_keys):
        q2 = jnp.concatenate([q_ref[0, :, g * AT_HEAD_DIM:(g + 1) * AT_HEAD_DIM] for g in range(AT_GROUP)], axis=0)
        m = jnp.full((AT_GROUP * TILE, 1), -jnp.inf, F32)
        acc = jnp.zeros((AT_GROUP * TILE, 2 * AT_HEAD_DIM), F32)
        bounds = list(range(0, n_keys, ATT_KEY_CHUNK))
        bounds = bounds[:max(len(bounds) - 1, 1)] if n_keys % ATT_KEY_CHUNK else bounds
        bounds.append(n_keys)
        for c0, c1 in zip(bounds[:-1], bounds[1:]):
            s = _dot_nt(q2, k_ref[0, c0:c1, :])
            m_new = jnp.maximum(m, jnp.max(s, axis=-1, keepdims=True))
            p = jnp.exp((s - m_new).astype(BF16))
            acc = acc * jnp.exp(m - m_new) + _dot(p, v_ref[0, c0:c1, :])
            m = m_new
        o = acc[:, 0:AT_HEAD_DIM] / acc[:, AT_HEAD_DIM:AT_HEAD_DIM + 1]
        for g in range(AT_GROUP):
            o_ref[0, :, g * AT_HEAD_DIM:(g + 1) * AT_HEAD_DIM] = o[g * TILE:(g + 1) * TILE]

    @pl.when(j == 0)
    def _():
        attend(TILE)

    @pl.when(j > 0)
    def _():
        attend(k_ref.shape[1])


def _attention(qn, kn, vb):
    batch, rows, _ = qn.shape
    nt = rows // TILE
    gw = AT_GROUP * AT_HEAD_DIM
    return pl.pallas_call(
        _attn_kernel,
        grid=(batch, AT_KV_HEADS, nt),
        in_specs=[pl.BlockSpec((1, TILE, gw), lambda b, h, j: (b, j, h)),
                  pl.BlockSpec((1, rows, AT_HEAD_DIM), lambda b, h, j: (b, 0, h)),
                  pl.BlockSpec((1, rows, 2 * AT_HEAD_DIM), lambda b, h, j: (b, 0, h))],
        out_specs=pl.BlockSpec((1, TILE, gw), lambda b, h, j: (b, j, h)),
        out_shape=jax.ShapeDtypeStruct((batch, rows, AT_HEADS * AT_HEAD_DIM), F32),
        compiler_params=_cparams(3, VMEM_LIMIT),
        name="attention",
    )(qn, kn, vb)


def _outproj_kernel(x_ref, mod_ref, ohf_ref, ohb_ref, gh_ref, omf_ref, omb_ref, gm_ref, at_ref,
                    hgg_ref, mlg_ref, ones_ref, wo_ref, g2_ref, wr_ref, br_ref, low_ref, up_ref,
                    xo_ref, st_ref, pos_ref, gate_ref, tab_ref, cnt_ref, cnt_scr):
    b = pl.program_id(0)
    j = pl.program_id(1)

    @pl.when((b == 0) & (j == 0))
    def _():
        cnt_scr[...] = jnp.zeros_like(cnt_scr)

    mod = mod_ref[0]
    gate1 = mod[:, 2 * D_MODEL:3 * D_MODEL]
    sh2 = mod[:, 3 * D_MODEL:4 * D_MODEL]
    sc2 = mod[:, 4 * D_MODEL:5 * D_MODEL]

    def head_norm(o, g):
        ms = _dot((o * o).astype(BF16), ones_ref[...]) * (1.0 / HEAD_DIM)
        return o * lax.rsqrt(ms + NORM_EPS) * g

    gh = gh_ref[0]
    y_hg = head_norm(ohf_ref[0, 0] + ohb_ref[0, 0], hgg_ref[...]) * (gh / (1.0 + jnp.exp(-gh)))
    y_ml = head_norm(omf_ref[0, 0] + omb_ref[0, 0], mlg_ref[...]) / (1.0 + jnp.exp(-gm_ref[0]))
    proj = (_dot(y_hg.astype(BF16), wo_ref[0:MIX_W, :])
            + _dot(y_ml.astype(BF16), wo_ref[MIX_W:2 * MIX_W, :])
            + _dot(at_ref[0].astype(BF16), wo_ref[2 * MIX_W:, :]))
    x = x_ref[0] + gate1 * proj
    xo_ref[0] = x
    h2 = _rms(x) * g2_ref[...]
    h2 = h2 * (1.0 + sc2) + sh2

    lane = lax.broadcasted_iota(I32, (TILE, LANES), 1)
    lane_f = lane.astype(F32)
    h_hi = h2.astype(BF16)
    h_lo = (h2 - h_hi.astype(F32)).astype(BF16)
    logits = _dot(jnp.concatenate([h_hi, h_lo, h_hi], axis=1), wr_ref[...]) + br_ref[...]
    lg = jnp.where(lane < N_EXPERTS, logits, -jnp.inf)
    multi = jnp.zeros((TILE, LANES), F32)
    hots, vals = [], []
    for _ in range(TOP_K):
        mx = jnp.max(lg, axis=-1, keepdims=True)
        idx = jnp.min(jnp.where(lg == mx, lane_f, F32(LANES)), axis=-1, keepdims=True)
        hot = lane_f == idx
        lg = jnp.where(hot, -jnp.inf, lg)
        multi = multi + hot.astype(F32)
        hots.append(hot)
        vals.append(mx)
    es = [jnp.exp(v - vals[0]) for v in vals]
    den = es[0] + es[1] + es[2] + es[3]

    tile_cnt = jnp.sum(multi, axis=0, keepdims=True)
    tile_start = _dot(jnp.broadcast_to(tile_cnt, (8, LANES)).astype(BF16), up_ref[...])[0:1]
    where_in_tile = _dot(low_ref[...], multi.astype(BF16)) + tile_start
    slot_lane = lax.broadcasted_iota(I32, (TILE, TOP_K * TILE), 1).astype(F32)
    onehot = jnp.zeros((TILE, TOP_K * TILE), F32)
    pos_out = jnp.zeros((TILE, LANES), I32)
    gate_out = jnp.zeros((TILE, LANES), F32)
    for kk in range(TOP_K):
        pos = jnp.sum(jnp.where(hots[kk], where_in_tile, 0.0), axis=-1, keepdims=True)
        onehot = onehot + jnp.where(slot_lane == pos, 1.0, 0.0)
        pos_out = pos_out + jnp.where(lane == kk, pos.astype(I32), 0)
        gate_out = gate_out + jnp.where(lane == kk, es[kk] / den, 0.0)
    sorted_rows = _dot_tn(onehot.astype(BF16), h_hi)
    st_ref[...] = sorted_rows.astype(BF16).reshape(TOP_K * TILE, ROW_SUB, LANES)
    pos_ref[0] = pos_out[:, 0:2 * TOP_K]
    gate_ref[0] = gate_out
    tab_ref[0] = jnp.concatenate([tile_cnt, tile_start, cnt_scr[...], jnp.zeros((5, LANES), F32)], axis=0).astype(I32)
    cnt_scr[...] = cnt_scr[...] + tile_cnt
    cnt_ref[...] = cnt_scr[...]


def _outproj(xs, mod_l, o_hg, p_hg, o_ml, p_ml, att, hg_g, ml_g, w_out_b, g2, w_router_p, b_router_p):
    batch, rows, _ = xs.shape
    nt = rows // TILE
    tok = lambda w, c=0: pl.BlockSpec((1, TILE, w), lambda b, j: (b, j, c))
    dirs = lambda d: pl.BlockSpec((1, 1, TILE, MIX_W), lambda b, j: (d, b, j, 0))
    full = lambda r, c: pl.BlockSpec((r, c), lambda b, j: (0, 0))
    block_ones = np.kron(np.eye(N_HEADS), np.ones((HEAD_DIM, HEAD_DIM))).astype(np.float32)
    strict_low = np.tril(np.ones((TILE, TILE)), -1).astype(np.float32)
    strict_up = np.triu(np.ones((LANES, LANES)), 1).astype(np.float32)
    n_tiles = batch * nt
    return pl.pallas_call(
        _outproj_kernel,
        grid=(batch, nt),
        in_specs=[tok(D_MODEL), _mod_spec(batch),
                  dirs(0), dirs(1), tok(MIX_W, 4),
                  dirs(0), dirs(1), tok(MIX_W, 3),
                  tok(AT_HEADS * AT_HEAD_DIM),
                  full(1, MIX_W), full(1, MIX_W), full(MIX_W, MIX_W),
                  full(D_MODEL, D_MODEL), full(1, D_MODEL),
                  full(3 * D_MODEL, LANES), full(1, LANES), full(TILE, TILE), full(LANES, LANES)],
        out_specs=[tok(D_MODEL),
                   pl.BlockSpec((TOP_K * TILE, ROW_SUB, LANES), lambda b, j: (b * nt + j, 0, 0)),
                   tok(2 * TOP_K), tok(LANES),
                   pl.BlockSpec((1, 8, LANES), lambda b, j: (b * nt + j, 0, 0)),
                   pl.BlockSpec((1, LANES), lambda b, j: (0, 0))],
        out_shape=[jax.ShapeDtypeStruct((batch, rows, D_MODEL), F32),
                   jax.ShapeDtypeStruct((n_tiles * TOP_K * TILE, ROW_SUB, LANES), BF16),
                   jax.ShapeDtypeStruct((batch, rows, 2 * TOP_K), I32),
                   jax.ShapeDtypeStruct((batch, rows, LANES), F32),
                   jax.ShapeDtypeStruct((n_tiles, 8, LANES), I32),
                   jax.ShapeDtypeStruct((1, LANES), F32)],
        scratch_shapes=[pltpu.VMEM((1, LANES), F32)],
        compiler_params=_cparams(2, VMEM_LIMIT),
        name="outproj_router",
    )(xs, mod_l, o_hg, o_hg, p_hg, o_ml, o_ml, p_ml, att, hg_g, ml_g, jnp.asarray(block_ones, dtype=BF16),
      w_out_b, g2, w_router_p, b_router_p, jnp.asarray(strict_low, dtype=BF16),
      jnp.asarray(strict_up, dtype=BF16))


_RUN_BITS = tuple(1 << s for s in range(TILE.bit_length() - 1, -1, -1))


def _copy_run(src, dst, sem, n, src0, dst0, wait):
    for bit in _RUN_BITS:
        done = n - (n & (2 * bit - 1))

        @pl.when((n & bit) != 0)
        def _():
            cp = pltpu.make_async_copy(src.at[pl.ds(src0 + done, bit)], dst.at[pl.ds(dst0 + done, bit)], sem)
            cp.wait() if wait else cp.start()


def _dispatch_kernel(n_ref, a_ref, b_ref, zn_ref, zo_ref, src_ref, dst_ref, zero_scr, sem):
    t = pl.program_id(0)

    @pl.when(t == 0)
    def _():
        zero_scr[...] = jnp.zeros_like(zero_scr)

    for wait in (False, True):
        def per_expert(e, carry):
            i = t * N_EXPERTS + e
            _copy_run(src_ref, dst_ref, sem, n_ref[i], a_ref[i], b_ref[i], wait)
            return carry

        lax.fori_loop(0, N_EXPERTS, per_expert, 0)

        @pl.when(t == 0)
        def _():
            def per_gap(g, carry):
                def per_piece(i, c):
                    left = jnp.minimum(zn_ref[g] - i * TILE, TILE)
                    _copy_run(zero_scr, dst_ref, sem, left, 0, zo_ref[g] + i * TILE, wait)
                    return c

                lax.fori_loop(0, (zn_ref[g] + TILE - 1) // TILE, per_piece, 0)
                return carry

            lax.fori_loop(0, zn_ref.shape[0], per_gap, 0)


def _dispatch(run_n, run_a, run_b, fill, h_tiles, n_slots):
    n_tiles = run_n.shape[0] // N_EXPERTS
    grid_spec = pltpu.PrefetchScalarGridSpec(
        num_scalar_prefetch=5,
        grid=(n_tiles,),
        in_specs=[pl.BlockSpec((TOP_K * TILE, ROW_SUB, LANES), lambda t, *_: (t, 0, 0))],
        out_specs=pl.BlockSpec(memory_space=pl.ANY),
        scratch_shapes=[pltpu.VMEM((TILE, ROW_SUB, LANES), h_tiles.dtype), pltpu.SemaphoreType.DMA(())],
    )
    return pl.pallas_call(
        _dispatch_kernel,
        grid_spec=grid_spec,
        out_shape=jax.ShapeDtypeStruct((n_slots, ROW_SUB, LANES), h_tiles.dtype),
        compiler_params=_cparams(1, VMEM_LIMIT),
        name="moe_dispatch",
    )(run_n, run_a, run_b, fill[0], fill[1], h_tiles)


def _expert_kernel(be_ref, nv_ref, x_ref, wgu_ref, bgu_ref, wd_ref, bd_ref, y_ref, wgu_scr, wd_scr):
    i = pl.program_id(0)

    @pl.when(i < nv_ref[0])
    def _():
        prev = be_ref[jnp.maximum(i - 1, 0)]

        @pl.when((i == 0) | (prev != be_ref[i]))
        def _():
            wgu_scr[...] = wgu_ref[0, 0].astype(BF16)
            wd_scr[...] = wd_ref[0, 0].astype(BF16)

        x = x_ref[...].reshape(MOE_BLOCK, D_MODEL)
        gu = _dot(x, wgu_scr[...]) + bgu_ref[0, 0]
        gate = jnp.minimum(gu[:, 0:D_EXPERT], SWIGLU_LIMIT)
        up = jnp.clip(gu[:, D_EXPERT:], -SWIGLU_LIMIT, SWIGLU_LIMIT)
        act = (up + 1.0) * gate * (1.0 / (1.0 + jnp.exp(-SWIGLU_ALPHA * gate)))
        y = _dot(act.astype(BF16), wd_scr[...]) + bd_ref[0, 0]
        y_ref[...] = y.astype(BF16).reshape(MOE_BLOCK, ROW_SUB, LANES)

    @pl.when(i >= nv_ref[0])
    def _():
        y_ref[...] = jnp.zeros_like(y_ref)


def _experts(layer, block_e, n_valid, xs_sorted, w_gu, b_gu, w_down, b_down):
    n_slots = xs_sorted.shape[0]
    nb = n_slots // MOE_BLOCK
    depth = w_gu.shape[0]
    clamp = lambda i, nv: jnp.minimum(i, jnp.maximum(nv[0] - 1, 0))
    grid_spec = pltpu.PrefetchScalarGridSpec(
        num_scalar_prefetch=2,
        grid=(nb,),
        in_specs=[pl.BlockSpec((MOE_BLOCK, ROW_SUB, LANES), lambda i, be, nv: (clamp(i, nv), 0, 0)),
                  pl.BlockSpec((1, 1, D_MODEL, 2 * D_EXPERT), lambda i, be, nv: (layer, be[i], 0, 0)),
                  pl.BlockSpec((1, 1, 1, 2 * D_EXPERT), lambda i, be, nv: (layer, be[i], 0, 0)),
                  pl.BlockSpec((1, 1, D_EXPERT, D_MODEL), lambda i, be, nv: (layer, be[i], 0, 0)),
                  pl.BlockSpec((1, 1, 1, D_MODEL), lambda i, be, nv: (layer, be[i], 0, 0))],
        out_specs=pl.BlockSpec((MOE_BLOCK, ROW_SUB, LANES), lambda i, be, nv: (i, 0, 0)),
        scratch_shapes=[pltpu.VMEM((D_MODEL, 2 * D_EXPERT), BF16),
                        pltpu.VMEM((D_EXPERT, D_MODEL), BF16)],
    )
    return pl.pallas_call(
        _expert_kernel,
        grid_spec=grid_spec,
        out_shape=jax.ShapeDtypeStruct((n_slots, ROW_SUB, LANES), BF16),
        compiler_params=_cparams(1, VMEM_LIMIT),
        name="moe_experts",
    )(block_e, n_valid, xs_sorted, w_gu, b_gu.reshape(depth, N_EXPERTS, 1, 2 * D_EXPERT),
      w_down, b_down.reshape(depth, N_EXPERTS, 1, D_MODEL))


def _combine_kernel(n_ref, a_ref, b_ref, x_ref, mod_ref, gate_ref, pos_ref, y_ref, o_ref, buf, sem):
    t = pl.program_id(0) * pl.num_programs(1) + pl.program_id(1)
    for wait in (False, True):
        def per_expert(e, carry):
            i = t * N_EXPERTS + e
            _copy_run(y_ref, buf, sem, n_ref[i], b_ref[i], a_ref[i], wait)
            return carry

        lax.fori_loop(0, N_EXPERTS, per_expert, 0)

    gates = gate_ref[0]
    pos = pos_ref[0].astype(F32)
    slot_lane = lax.broadcasted_iota(I32, (TILE, TOP_K * TILE), 1).astype(F32)
    wmat = jnp.zeros((TILE, TOP_K * TILE), F32)
    for kk in range(TOP_K):
        wmat = wmat + jnp.where(slot_lane == pos[:, kk:kk + 1], gates[:, kk:kk + 1], 0.0)
    f = _dot(wmat.astype(BF16), buf[...].reshape(TOP_K * TILE, D_MODEL))
    gate2 = mod_ref[0][:, 5 * D_MODEL:6 * D_MODEL]
    o_ref[0] = x_ref[0] + gate2 * f


def _combine(run_n, run_a, run_b, xs, mod_l, gates, pos, y_sorted):
    batch, rows, _ = xs.shape
    nt = rows // TILE
    tok = lambda w: pl.BlockSpec((1, TILE, w), lambda b, j, *_: (b, j, 0))
    mod_spec = _mod_spec(batch)
    grid_spec = pltpu.PrefetchScalarGridSpec(
        num_scalar_prefetch=3,
        grid=(batch, nt),
        in_specs=[tok(D_MODEL),
                  pl.BlockSpec(mod_spec.block_shape, lambda b, j, *_: mod_spec.index_map(b, j)),
                  tok(LANES), tok(2 * TOP_K), pl.BlockSpec(memory_space=pl.ANY)],
        out_specs=tok(D_MODEL),
        scratch_shapes=[pltpu.VMEM((TOP_K * TILE, ROW_SUB, LANES), y_sorted.dtype), pltpu.SemaphoreType.DMA(())],
    )
    return pl.pallas_call(
        _combine_kernel,
        grid_spec=grid_spec,
        out_shape=jax.ShapeDtypeStruct((batch, rows, D_MODEL), F32),
        compiler_params=_cparams(2, VMEM_LIMIT),
        name="moe_combine",
    )(run_n, run_a, run_b, xs, mod_l, gates, pos, y_sorted)


def _moe_plan(counts, tables, n_tokens):
    cnt = counts[0, :N_EXPERTS].astype(I32)
    padded = (cnt + MOE_BLOCK - 1) // MOE_BLOCK * MOE_BLOCK
    pend = jnp.cumsum(padded)
    pstart = pend - padded
    nb = -(-(n_tokens * TOP_K) // MOE_BLOCK) + N_EXPERTS
    block_row = jnp.arange(nb, dtype=I32) * MOE_BLOCK
    block_e = jnp.minimum(jnp.sum((pend[None, :] <= block_row[:, None]).astype(I32), axis=1), N_EXPERTS - 1)
    n_valid = (pend[-1] // MOE_BLOCK).astype(I32).reshape(1)
    run_n = tables[:, 0, :N_EXPERTS]
    tile_pos = tables[:, 1, :N_EXPERTS]
    slot_pos = tables[:, 2, :N_EXPERTS] + pstart[None, :]
    fill_n = jnp.concatenate([padded - cnt, (nb * MOE_BLOCK - pend[-1])[None]])
    fill_at = jnp.concatenate([pstart + cnt, pend[-1:]])
    flat = lambda a: a.reshape(-1).astype(I32)
    return (flat(run_n), flat(tile_pos), flat(slot_pos), (flat(fill_n), flat(fill_at)),
            block_e.astype(I32), n_valid, nb * MOE_BLOCK)


def _arrange_w_in(w_in_l):
    hg = w_in_l[:, 0:HG_IN]
    ml0 = HG_IN
    ml = w_in_l[:, ml0:ml0 + ML_IN]
    gt = w_in_l[:, ml0 + ML_IN:ml0 + ML_IN + 4 * N_HEADS]
    at = w_in_l[:, ml0 + ML_IN + 4 * N_HEADS:]
    gt = jnp.repeat(_gate_order(gt.T).T, HEAD_DIM, axis=1)
    return jnp.concatenate([hg, ml, at, gt], 1).astype(BF16)


def _gate_order(g):
    g = g.reshape((4, N_HEADS) + g.shape[1:]) if g.shape[0] == 4 * N_HEADS else g
    return jnp.concatenate([g[0], g[2], g[1], g[3]], axis=0)


def _hgrn2_lower_bounds(lb_logits):
    p = jax.nn.softmax(lb_logits.astype(F32), axis=1)
    cum = jnp.cumsum(p, axis=1)
    return cum - cum[:, :1]


def kernel(x, c, ctx, c_ctx, w_mod, b_mod, norm1_g, norm2_g, w_in, w_out, hg_lb_logits, hg_norm_g,
           ml_gate_bias, ml_norm_g, q_norm_g, k_norm_g, w_router, b_router, w_gu, b_gu, w_down, b_down):
    batch, seq, d = x.shape
    ctx_len = ctx.shape[1]
    depth = w_mod.shape[0]
    assert d == D_MODEL and ctx_len == TILE and seq % TILE == 0 and batch + 1 <= MOD_ROWS
    rows = ctx_len + seq
    n_tokens = batch * rows

    cvec = jnp.concatenate([c, c_ctx[None, :], jnp.zeros((MOD_ROWS - batch - 1, d), F32)], 0)
    mod = _modulation(cvec, w_mod, b_mod).reshape(depth, MOD_ROWS, 1, N_MOD)
    hg_lb = _hgrn2_lower_bounds(hg_lb_logits)
    cos, sin = _rope_tables(rows, ctx_len)
    xs = jnp.concatenate([ctx, x], axis=1)

    for layer in range(depth):
        mod_l = mod[layer]
        p_hg, p_ml, p_at, p_gt = _inproj(xs, mod_l, norm1_g[layer][None, :], _arrange_w_in(w_in[layer]))
        o_hg = _hgrn2(p_hg, hg_lb[:, layer])
        gb = jnp.repeat(_gate_order(ml_gate_bias[layer]), HEAD_DIM).reshape(4, 1, MIX_W)
        o_ml = _mlstm(p_ml, p_gt, gb[0::2], gb[1::2])
        qn, kn, vb = _attn_prep(p_at, cos, sin, q_norm_g[layer][None, :], k_norm_g[layer][None, :])
        att = _attention(qn, kn, vb)
        w_router_p = jnp.pad(w_router[layer], ((0, 0), (0, LANES - N_EXPERTS)))
        wr_hi = w_router_p.astype(BF16)
        wr_lo = (w_router_p - wr_hi.astype(F32)).astype(BF16)
        w_router_p = jnp.concatenate([wr_hi, wr_hi, wr_lo], axis=0)
        b_router_p = jnp.pad(b_router[layer], (0, LANES - N_EXPERTS))[None, :]
        xs, h_tiles, pos, gates, tables, counts = _outproj(
            xs, mod_l, o_hg, p_hg, o_ml, p_ml, att,
            jnp.tile(hg_norm_g[layer], N_HEADS)[None, :], jnp.tile(ml_norm_g[layer], N_HEADS)[None, :],
            w_out[layer].astype(BF16), norm2_g[layer][None, :], w_router_p, b_router_p)
        run_n, tile_pos, slot_pos, fill, block_e, n_valid, n_slots = _moe_plan(counts, tables, n_tokens)
        xs_sorted = _dispatch(run_n, tile_pos, slot_pos, fill, h_tiles, n_slots)
        y_sorted = _experts(layer, block_e, n_valid, xs_sorted, w_gu, b_gu, w_down, b_down)
        xs = _combine(run_n, tile_pos, slot_pos, xs, mod_l, gates, pos, y_sorted)
    return xs[:, ctx_len:, :]
```

```python
import functools

import numpy as np
import jax
import jax.numpy as jnp
from jax import lax
from jax.experimental import pallas as pl
from jax.experimental.pallas import tpu as pltpu

F32 = jnp.float32
BF16 = jnp.bfloat16
I32 = jnp.int32
HIGHEST = lax.Precision.HIGHEST

D_MODEL = 1024
NORM_EPS = 1e-6
CHUNK = 64
TILE = 256
CHUNKS_PER_TILE = TILE // CHUNK
GRID_W = 64
ROPE_THETA = 10000.0

N_HEADS = 4
HEAD_DIM = 64
MIX_W = N_HEADS * HEAD_DIM
AT_HEADS = 4
AT_KV_HEADS = 2
AT_HEAD_DIM = 128
AT_GROUP = AT_HEADS // AT_KV_HEADS
ATT_KEY_CHUNK = 2176

HG_IN = 5 * MIX_W
ML_IN = 4 * MIX_W
AT_IN = (AT_HEADS + 2 * AT_KV_HEADS) * AT_HEAD_DIM
GATE_W = 4 * MIX_W

N_EXPERTS = 32
TOP_K = 4
D_EXPERT = 1024
SWIGLU_LIMIT = 7.0
SWIGLU_ALPHA = 1.702
MOE_BLOCK = 512
ROW_SUB = D_MODEL // 128
LANES = 128
N_MOD = 6 * D_MODEL
MOD_ROWS = 16

VMEM_LIMIT = 56 * 1024 * 1024


def _cparams(n_axes, vmem=None):
    kw = dict(dimension_semantics=("arbitrary",) * n_axes)
    if vmem is not None:
        kw["vmem_limit_bytes"] = vmem
    return pltpu.CompilerParams(**kw)


def _dot(a, b, precision=None):
    return jnp.dot(a, b, preferred_element_type=F32, precision=precision)


def _dot_nt(a, b):
    return lax.dot_general(a, b, (((1,), (1,)), ((), ())), preferred_element_type=F32)


def _dot_tn(a, b, precision=None):
    return lax.dot_general(a, b, (((0,), (0,)), ((), ())), preferred_element_type=F32,
                           precision=precision)


def _log_sigmoid(z):
    return jnp.minimum(z, 0.0) - jnp.log1p(jnp.exp(-jnp.abs(z)))


def _rms(x, eps=NORM_EPS):
    return x * lax.rsqrt(jnp.mean(x * x, axis=-1, keepdims=True) + eps)


def _mod_kernel(c_ref, w_ref, b_ref, o_ref):
    cv = c_ref[...]
    s = cv * (1.0 / (1.0 + jnp.exp(-cv)))
    o_ref[0] = _dot(s, w_ref[0], precision=HIGHEST) + b_ref[0]


def _modulation(cvec, w_mod, b_mod):
    depth = w_mod.shape[0]
    nblk = 1536
    return pl.pallas_call(
        _mod_kernel,
        grid=(depth, N_MOD // nblk),
        in_specs=[pl.BlockSpec((MOD_ROWS, D_MODEL), lambda l, n: (0, 0)),
                  pl.BlockSpec((1, D_MODEL, nblk), lambda l, n: (l, 0, n)),
                  pl.BlockSpec((1, 1, nblk), lambda l, n: (l, 0, n))],
        out_specs=pl.BlockSpec((1, MOD_ROWS, nblk), lambda l, n: (l, 0, n)),
        out_shape=jax.ShapeDtypeStruct((depth, MOD_ROWS, N_MOD), F32),
        compiler_params=_cparams(2, VMEM_LIMIT),
        name="modulation",
    )(cvec, w_mod, b_mod.reshape(depth, 1, N_MOD))


def _mod_spec(batch):
    return pl.BlockSpec((1, 1, N_MOD), lambda b, j: (jnp.where(j == 0, batch, b), 0, 0))


def _inproj_kernel(x_ref, mod_ref, g_ref, w_ref, cos_ref, sin_ref, qg_ref, kg_ref,
                   hg_ref, ml_ref, gt_ref, q_ref, k_ref, v_ref):
    x = x_ref[0]
    mod = mod_ref[0]
    sh = mod[:, 0:D_MODEL]
    sc = mod[:, D_MODEL:2 * D_MODEL]
    h = _rms(x) * g_ref[...]
    h = (h * (1.0 + sc) + sh).astype(BF16)
    o0 = 0
    for ref, width in ((hg_ref, HG_IN), (ml_ref, ML_IN), (gt_ref, GATE_W)):
        ref[0] = _dot(h, w_ref[:, o0:o0 + width])
        o0 += width

    at = _dot(h, w_ref[:, o0:o0 + AT_IN])
    cos = cos_ref[...]
    sin = sin_ref[...]
    lane = lax.broadcasted_iota(I32, (TILE, AT_HEAD_DIM), 1)
    quarter = AT_HEAD_DIM // 4
    first_half = (lane % (2 * quarter)) < quarter

    def norm_rope(y, g):
        y = _rms(y) * g
        partner = jnp.where(first_half, pltpu.roll(y, AT_HEAD_DIM - quarter, axis=1), pltpu.roll(y, quarter, axis=1))
        return y * cos + partner * sin

    head = lambda i: at[:, i * AT_HEAD_DIM:(i + 1) * AT_HEAD_DIM]
    for i in range(AT_HEADS):
        q_ref[0, :, i * AT_HEAD_DIM:(i + 1) * AT_HEAD_DIM] = (
            norm_rope(head(i), qg_ref[...]) * (AT_HEAD_DIM ** -0.5)).astype(BF16)
    for i in range(AT_KV_HEADS):
        k_ref[0, :, i * AT_HEAD_DIM:(i + 1) * AT_HEAD_DIM] = norm_rope(head(AT_HEADS + i), kg_ref[...]).astype(BF16)
        v_ref[0, :, 2 * i * AT_HEAD_DIM:(2 * i + 1) * AT_HEAD_DIM] = head(AT_HEADS + AT_KV_HEADS + i).astype(BF16)
        v_ref[0, :, (2 * i + 1) * AT_HEAD_DIM:(2 * i + 2) * AT_HEAD_DIM] = jnp.ones((TILE, AT_HEAD_DIM), BF16)


def _inproj(xs, mod_l, g1, w_in_r, cos, sin, q_g, k_g):
    batch, rows, _ = xs.shape
    nt = rows // TILE
    n_all = HG_IN + ML_IN + GATE_W + AT_IN
    tok = lambda w: pl.BlockSpec((1, TILE, w), lambda b, j: (b, j, 0))
    tab = pl.BlockSpec((TILE, AT_HEAD_DIM), lambda b, j: (j, 0))
    vec = pl.BlockSpec((1, AT_HEAD_DIM), lambda b, j: (0, 0))
    kvw = AT_KV_HEADS * AT_HEAD_DIM
    widths = ((HG_IN, F32), (ML_IN, F32), (GATE_W, F32), (AT_HEADS * AT_HEAD_DIM, BF16), (kvw, BF16), (2 * kvw, BF16))
    return pl.pallas_call(
        _inproj_kernel,
        grid=(batch, nt),
        in_specs=[tok(D_MODEL), _mod_spec(batch),
                  pl.BlockSpec((1, D_MODEL), lambda b, j: (0, 0)),
                  pl.BlockSpec((D_MODEL, n_all), lambda b, j: (0, 0)),
                  tab, tab, vec, vec],
        out_specs=[tok(w) for w, _ in widths],
        out_shape=[jax.ShapeDtypeStruct((batch, rows, w), dt) for w, dt in widths],
        compiler_params=_cparams(2, VMEM_LIMIT),
        name="inproj",
    )(xs, mod_l, g1, w_in_r, cos, sin, q_g, k_g)


def _tile_of(d, j, nt):
    return j + d * jnp.where(j > 0, nt - 2 * j, 0)


def _chunk_of(d, i):
    return i + d * (CHUNKS_PER_TILE - 1 - 2 * i)


_HG_LEVELS = (32, 16, 8, 4, 2, 1)


def _head_block_mask():
    h = np.arange(MIX_W) // HEAD_DIM
    return (h[:, None] == h[None, :]).astype(np.float32)


def _split_hi_lo(x):
    hi = x.astype(BF16)
    lo = (x - hi.astype(F32)).astype(BF16)
    return jnp.concatenate([hi, lo], axis=0)


def _block_diag_rows(x, lane_head):
    zero = jnp.zeros_like(x)
    return jnp.concatenate([jnp.where(lane_head == h, x, zero) for h in range(N_HEADS)], axis=0)


@functools.lru_cache(maxsize=None)
def _hgrn2_consts():
    c = CHUNK
    tri = np.tril(np.ones((c, c)))
    ops = [tri]
    masks = []
    pos = np.arange(c)
    for m in _HG_LEVELS:
        blk = pos // (2 * m)
        sel = np.zeros((c, c))
        sel[pos, 2 * m * blk + m - 1] = 1.0
        ops.append(tri - sel @ tri)
        in_b = (pos % (2 * m)) >= m
        masks.append(((blk[:, None] == blk[None, :]) & in_b[:, None] & ~in_b[None, :]).astype(np.float64))
    masks.append(np.eye(c))
    ops.append(np.ones((c, c)) - tri)
    flip = np.eye(c)[::-1]
    stack = lambda xs: np.concatenate(xs, 0)
    ops_d = np.stack([stack(ops), stack([flip @ o @ flip for o in ops])])
    ops_d = np.concatenate([ops_d, ops_d], axis=2)
    tile = lambda mk: np.tile(mk, (1, N_HEADS))
    masks_d = np.stack([np.stack([tile(mk) for mk in masks]), np.stack([tile(mk[::-1, ::-1]) for mk in masks])])
    return ops_d.astype(np.float32), masks_d.astype(np.float32)


def _hgrn2_kernel(q_ref, z_ref, v_ref, lb_ref, ops_ref, msk_ref, hm_ref, o_ref, st_scr, lf_scr, k_scr):
    d = pl.program_id(1)
    j = pl.program_id(2)

    @pl.when(j == 0)
    def _():
        st_scr[...] = jnp.zeros_like(st_scr)

    lb = lb_ref[0]
    z = z_ref[0]
    a = jnp.log(lb)
    cterm = jnp.log1p(-lb) + _log_sigmoid(z)
    lf_scr[...] = jnp.maximum(a, cterm) + jnp.log1p(jnp.exp(-jnp.abs(a - cterm)))
    k_scr[...] = (1.0 - lb) / (1.0 + jnp.exp(z))
    n_lv = len(_HG_LEVELS)
    lane_head = lax.broadcasted_iota(I32, (CHUNK, MIX_W), 1) // HEAD_DIM

    for i in range(CHUNKS_PER_TILE):
        off = pl.multiple_of(_chunk_of(d, i) * CHUNK, CHUNK)
        rows = pl.ds(off, CHUNK)
        cs = _dot(ops_ref[0], _split_hi_lo(lf_scr[rows, :]))
        b = cs[0:CHUNK]
        rest = cs[(n_lv + 1) * CHUNK:(n_lv + 2) * CHUNK]
        q = q_ref[0, rows, :]
        k = k_scr[rows, :]
        vb = v_ref[0, rows, :].astype(BF16)
        q_in = (q * jnp.exp(b)).astype(BF16)
        k_out = (k * jnp.exp(rest)).astype(BF16)
        s_decay = jnp.exp(b[0:1] + rest[0:1])
        sc = jnp.zeros((CHUNK, MIX_W), F32)
        for lv in range(n_lv + 1):
            if lv < n_lv:
                e = jnp.exp(-jnp.abs(cs[(lv + 1) * CHUNK:(lv + 2) * CHUNK]))
                ql = (q * e).astype(BF16)
                kl = (k * e).astype(BF16)
            else:
                ql = q.astype(BF16)
                kl = k.astype(BF16)
            sc = sc + msk_ref[0, lv] * _dot_nt(ql, _block_diag_rows(kl, lane_head))
        st = st_scr[...]
        o_ref[0, 0, rows, :] = (_dot(sc.astype(BF16), _block_diag_rows(vb, lane_head))
                                + _dot_nt(q_in, st.astype(BF16)))
        st_scr[...] = st * s_decay + hm_ref[...] * _dot_tn(vb, k_out)


def _hgrn2(p_hg, lb_l):
    batch, rows, _ = p_hg.shape
    nt = rows // TILE
    ops_d, masks_d = _hgrn2_consts()
    n_ops = ops_d.shape[1]
    n_msk = masks_d.shape[1]
    col = lambda c: pl.BlockSpec((1, TILE, MIX_W), lambda b, d, j: (b, _tile_of(d, j, nt), c))
    return pl.pallas_call(
        _hgrn2_kernel,
        grid=(batch, 2, nt),
        in_specs=[col(0),
                  pl.BlockSpec((1, TILE, MIX_W), lambda b, d, j: (b, _tile_of(d, j, nt), 1 + d)),
                  col(3),
                  pl.BlockSpec((1, 1, MIX_W), lambda b, d, j: (d, 0, 0)),
                  pl.BlockSpec((1, n_ops, 2 * CHUNK), lambda b, d, j: (d, 0, 0)),
                  pl.BlockSpec((1, n_msk, CHUNK, MIX_W), lambda b, d, j: (d, 0, 0, 0)),
                  pl.BlockSpec((MIX_W, MIX_W), lambda b, d, j: (0, 0))],
        out_specs=pl.BlockSpec((1, 1, TILE, MIX_W), lambda b, d, j: (d, b, _tile_of(d, j, nt), 0)),
        out_shape=jax.ShapeDtypeStruct((2, batch, rows, MIX_W), F32),
        scratch_shapes=[pltpu.VMEM((MIX_W, MIX_W), F32),
                        pltpu.VMEM((TILE, MIX_W), F32),
                        pltpu.VMEM((TILE, MIX_W), F32)],
        compiler_params=_cparams(3, VMEM_LIMIT),
        name="hgrn2",
    )(p_hg, p_hg, p_hg, lb_l.reshape(2, 1, MIX_W), jnp.asarray(ops_d, dtype=BF16), jnp.asarray(masks_d),
      jnp.asarray(_head_block_mask()))


@functools.lru_cache(maxsize=None)
def _mlstm_consts():
    c = CHUNK
    tri = np.tril(np.ones((c, c)))
    flip = np.eye(c)[::-1]
    tile = lambda mk: np.tile(mk, (1, N_HEADS))
    ops, masks = [], []
    for t in (tri, flip @ tri @ flip):
        ops.append(np.concatenate([t, t], axis=1))
        masks.append(np.stack([tile(t.T), tile(t), tile(np.eye(c))]))
    return np.stack(ops).astype(np.float32), np.stack(masks).astype(np.float32)


def _mlstm_kernel(q_ref, k_ref, v_ref, gi_ref, gf_ref, bi_ref, bf_ref, op_ref, cm_ref, hm_ref, h_ref,
                  ct_scr, n_scr, m_scr, i_scr, f_scr):
    d = pl.program_id(1)
    j = pl.program_id(2)

    @pl.when(j == 0)
    def _():
        ct_scr[...] = jnp.zeros_like(ct_scr)
        n_scr[...] = jnp.zeros_like(n_scr)
        m_scr[...] = jnp.zeros_like(m_scr)

    i_scr[...] = gi_ref[0] + bi_ref[0]
    f_scr[...] = _log_sigmoid(gf_ref[0] + bf_ref[0])
    lane_head = lax.broadcasted_iota(I32, (CHUNK, MIX_W), 1) // HEAD_DIM
    neg_inf = F32(-jnp.inf)
    hm = hm_ref[...]
    hm_b = hm.astype(BF16)

    for i in range(CHUNKS_PER_TILE):
        off = pl.multiple_of(_chunk_of(d, i) * CHUNK, CHUNK)
        rows = pl.ds(off, CHUNK)
        ie = i_scr[rows, :]
        fe = f_scr[rows, :]
        bc = _dot(op_ref[0], _split_hi_lo(fe))
        row = jnp.sum(ie * cm_ref[0, 2] - fe * cm_ref[0, 0], axis=0, keepdims=True)
        tot = jnp.sum(fe, axis=0, keepdims=True)
        m_st = m_scr[...]
        dm = jnp.where(cm_ref[0, 1] > 0.5, bc + row, neg_inf)
        m_in = jnp.zeros((CHUNK, MIX_W), F32)
        for h in range(N_HEADS):
            mh = jnp.max(dm[:, h * HEAD_DIM:(h + 1) * HEAD_DIM], axis=-1, keepdims=True)
            m_in = jnp.where(lane_head == h, mh, m_in)
        from_prev = bc + m_st
        m_t = jnp.maximum(from_prev, m_in)
        q = q_ref[0, rows, :]
        k = k_ref[0, rows, :] * (HEAD_DIM ** -0.5)
        qb = q.astype(BF16)
        vb = v_ref[0, rows, :].astype(BF16)
        w = jnp.exp(dm - m_t) * _dot_nt(qb, _block_diag_rows(k.astype(BF16), lane_head))
        a_prev = jnp.exp(from_prev - m_t)
        ct = ct_scr[...]
        n_st = n_scr[...]
        num = _dot(w.astype(BF16), _block_diag_rows(vb, lane_head)) + a_prev * _dot_nt(qb, ct.astype(BF16))
        den = _dot((w + a_prev * (q * n_st)).astype(BF16), hm_b)
        h_ref[0, 0, rows, :] = num / jnp.maximum(jnp.abs(den), jnp.exp(-m_t))
        e_col = tot - bc + ie
        m_new = jnp.maximum(tot + m_st, jnp.max(e_col, axis=0, keepdims=True))
        a_end = jnp.exp(tot + m_st - m_new)
        kw = k * jnp.exp(e_col - m_new)
        ct_scr[...] = ct * a_end + hm * _dot_tn(vb, kw.astype(BF16))
        n_scr[...] = a_end * n_st + jnp.sum(kw, axis=0, keepdims=True)
        m_scr[...] = m_new


def _mlstm(p_ml, p_gt, bias_i, bias_f):
    batch, rows, _ = p_ml.shape
    nt = rows // TILE
    ops_d, masks_d = _mlstm_consts()
    col = lambda c: pl.BlockSpec((1, TILE, MIX_W), lambda b, d, j: (b, _tile_of(d, j, nt), c))
    gate = lambda kind: pl.BlockSpec((1, TILE, MIX_W), lambda b, d, j: (b, _tile_of(d, j, nt), 2 * d + kind))
    by_dir = lambda shp: pl.BlockSpec((1,) + shp, lambda b, d, j: (d,) + (0,) * len(shp))
    return pl.pallas_call(
        _mlstm_kernel,
        grid=(batch, 2, nt),
        in_specs=[col(0), col(1), col(2), gate(0), gate(1),
                  by_dir((1, MIX_W)), by_dir((1, MIX_W)),
                  by_dir((CHUNK, 2 * CHUNK)), by_dir((3, CHUNK, MIX_W)),
                  pl.BlockSpec((MIX_W, MIX_W), lambda b, d, j: (0, 0))],
        out_specs=pl.BlockSpec((1, 1, TILE, MIX_W), lambda b, d, j: (d, b, _tile_of(d, j, nt), 0)),
        out_shape=jax.ShapeDtypeStruct((2, batch, rows, MIX_W), F32),
        scratch_shapes=[pltpu.VMEM((MIX_W, MIX_W), F32),
                        pltpu.VMEM((1, MIX_W), F32),
                        pltpu.VMEM((1, MIX_W), F32),
                        pltpu.VMEM((TILE, MIX_W), F32),
                        pltpu.VMEM((TILE, MIX_W), F32)],
        compiler_params=_cparams(3, VMEM_LIMIT),
        name="mlstm",
    )(p_ml, p_ml, p_ml, p_gt, p_gt, bias_i, bias_f, jnp.asarray(ops_d, dtype=BF16), jnp.asarray(masks_d),
      jnp.asarray(_head_block_mask()))


def _rope_tables(rows, ctx_len):
    n = rows - ctx_len
    r = jnp.arange(n) // GRID_W
    cc = jnp.arange(n) % GRID_W
    axis_dims = AT_HEAD_DIM // 2
    inv_freq = ROPE_THETA ** (-jnp.arange(0, axis_dims, 2, dtype=F32) / axis_dims)
    ang_r = r.astype(F32)[:, None] * inv_freq
    ang_c = cc.astype(F32)[:, None] * inv_freq
    cos = jnp.concatenate([jnp.cos(ang_r), jnp.cos(ang_r), jnp.cos(ang_c), jnp.cos(ang_c)], -1)
    sin = jnp.concatenate([-jnp.sin(ang_r), jnp.sin(ang_r), -jnp.sin(ang_c), jnp.sin(ang_c)], -1)
    cos = jnp.concatenate([jnp.ones((ctx_len, AT_HEAD_DIM), F32), cos], 0)
    sin = jnp.concatenate([jnp.zeros((ctx_len, AT_HEAD_DIM), F32), sin], 0)
    return cos, sin


def _attn_kernel(q_ref, k_ref, v_ref, o_ref):
    j = pl.program_id(2)

    def attend(n_keys):
        q2 = jnp.concatenate([q_ref[0, :, g * AT_HEAD_DIM:(g + 1) * AT_HEAD_DIM] for g in range(AT_GROUP)], axis=0)
        m = jnp.full((AT_GROUP * TILE, 1), -jnp.inf, F32)
        acc = jnp.zeros((AT_GROUP * TILE, 2 * AT_HEAD_DIM), F32)
        bounds = list(range(0, n_keys, ATT_KEY_CHUNK))
        bounds = bounds[:max(len(bounds) - 1, 1)] if n_keys % ATT_KEY_CHUNK else bounds
        bounds.append(n_keys)
        for c0, c1 in zip(bounds[:-1], bounds[1:]):
            s = _dot_nt(q2, k_ref[0, c0:c1, :])
            m_new = jnp.maximum(m, jnp.max(s, axis=-1, keepdims=True))
            p = jnp.exp((s - m_new).astype(BF16))
            acc = acc * jnp.exp(m - m_new) + _dot(p, v_ref[0, c0:c1, :])
            m = m_new
        o = acc[:, 0:AT_HEAD_DIM] / acc[:, AT_HEAD_DIM:AT_HEAD_DIM + 1]
        for g in range(AT_GROUP):
            o_ref[0, :, g * AT_HEAD_DIM:(g + 1) * AT_HEAD_DIM] = o[g * TILE:(g + 1) * TILE]

    @pl.when(j == 0)
    def _():
        attend(TILE)

    @pl.when(j > 0)
    def _():
        attend(k_ref.shape[1])


def _attention(qn, kn, vb):
    batch, rows, _ = qn.shape
    nt = rows // TILE
    gw = AT_GROUP * AT_HEAD_DIM
    return pl.pallas_call(
        _attn_kernel,
        grid=(batch, AT_KV_HEADS, nt),
        in_specs=[pl.BlockSpec((1, TILE, gw), lambda b, h, j: (b, j, h)),
                  pl.BlockSpec((1, rows, AT_HEAD_DIM), lambda b, h, j: (b, 0, h)),
                  pl.BlockSpec((1, rows, 2 * AT_HEAD_DIM), lambda b, h, j: (b, 0, h))],
        out_specs=pl.BlockSpec((1, TILE, gw), lambda b, h, j: (b, j, h)),
        out_shape=jax.ShapeDtypeStruct((batch, rows, AT_HEADS * AT_HEAD_DIM), F32),
        compiler_params=_cparams(3, VMEM_LIMIT),
        name="attention",
    )(qn, kn, vb)


def _outproj_kernel(x_ref, mod_ref, ohf_ref, ohb_ref, gh_ref, omf_ref, omb_ref, gm_ref, at_ref,
                    hgg_ref, mlg_ref, ones_ref, wo_ref, g2_ref, wr_ref, br_ref, low_ref, up_ref,
                    xo_ref, st_ref, pos_ref, gate_ref, tab_ref, cnt_ref, cnt_scr):
    b = pl.program_id(0)
    j = pl.program_id(1)

    @pl.when((b == 0) & (j == 0))
    def _():
        cnt_scr[...] = jnp.zeros_like(cnt_scr)

    mod = mod_ref[0]
    gate1 = mod[:, 2 * D_MODEL:3 * D_MODEL]
    sh2 = mod[:, 3 * D_MODEL:4 * D_MODEL]
    sc2 = mod[:, 4 * D_MODEL:5 * D_MODEL]

    def head_norm(o, g):
        ms = _dot((o * o).astype(BF16), ones_ref[...]) * (1.0 / HEAD_DIM)
        return o * lax.rsqrt(ms + NORM_EPS) * g

    gh = gh_ref[0]
    y_hg = head_norm(ohf_ref[0, 0] + ohb_ref[0, 0], hgg_ref[...]) * (gh / (1.0 + jnp.exp(-gh)))
    y_ml = head_norm(omf_ref[0, 0] + omb_ref[0, 0], mlg_ref[...]) / (1.0 + jnp.exp(-gm_ref[0]))
    proj = (_dot(y_hg.astype(BF16), wo_ref[0:MIX_W, :])
            + _dot(y_ml.astype(BF16), wo_ref[MIX_W:2 * MIX_W, :])
            + _dot(at_ref[0].astype(BF16), wo_ref[2 * MIX_W:, :]))
    x = x_ref[0] + gate1 * proj
    xo_ref[0] = x
    h2 = _rms(x) * g2_ref[...]
    h2 = h2 * (1.0 + sc2) + sh2

    lane = lax.broadcasted_iota(I32, (TILE, LANES), 1)
    lane_f = lane.astype(F32)
    h_hi = h2.astype(BF16)
    h_lo = (h2 - h_hi.astype(F32)).astype(BF16)
    logits = _dot(jnp.concatenate([h_hi, h_lo, h_hi], axis=1), wr_ref[...]) + br_ref[...]
    lg = jnp.where(lane < N_EXPERTS, logits, -jnp.inf)
    multi = jnp.zeros((TILE, LANES), F32)
    hots, vals = [], []
    for _ in range(TOP_K):
        mx = jnp.max(lg, axis=-1, keepdims=True)
        idx = jnp.min(jnp.where(lg == mx, lane_f, F32(LANES)), axis=-1, keepdims=True)
        hot = lane_f == idx
        lg = jnp.where(hot, -jnp.inf, lg)
        multi = multi + hot.astype(F32)
        hots.append(hot)
        vals.append(mx)
    es = [jnp.exp(v - vals[0]) for v in vals]
    den = es[0] + es[1] + es[2] + es[3]

    tile_cnt = jnp.sum(multi, axis=0, keepdims=True)
    tile_start = _dot(jnp.broadcast_to(tile_cnt, (8, LANES)).astype(BF16), up_ref[...])[0:1]
    where_in_tile = _dot(low_ref[...], multi.astype(BF16)) + tile_start
    slot_lane = lax.broadcasted_iota(I32, (TILE, TOP_K * TILE), 1).astype(F32)
    onehot = jnp.zeros((TILE, TOP_K * TILE), F32)
    pos_out = jnp.zeros((TILE, LANES), I32)
    gate_out = jnp.zeros((TILE, LANES), F32)
    for kk in range(TOP_K):
        pos = jnp.sum(jnp.where(hots[kk], where_in_tile, 0.0), axis=-1, keepdims=True)
        onehot = onehot + jnp.where(slot_lane == pos, 1.0, 0.0)
        pos_out = pos_out + jnp.where(lane == kk, pos.astype(I32), 0)
        gate_out = gate_out + jnp.where(lane == kk, es[kk] / den, 0.0)
    sorted_rows = _dot_tn(onehot.astype(BF16), h_hi)
    st_ref[...] = sorted_rows.astype(BF16).reshape(TOP_K * TILE, ROW_SUB, LANES)
    pos_ref[0] = pos_out[:, 0:2 * TOP_K]
    gate_ref[0] = gate_out
    tab_ref[0] = jnp.concatenate([tile_cnt, tile_start, cnt_scr[...], jnp.zeros((5, LANES), F32)], axis=0).astype(I32)
    cnt_scr[...] = cnt_scr[...] + tile_cnt
    cnt_ref[...] = cnt_scr[...]


def _outproj(xs, mod_l, o_hg, p_hg, o_ml, p_ml, att, hg_g, ml_g, w_out_b, g2, w_router_p, b_router_p):
    batch, rows, _ = xs.shape
    nt = rows // TILE
    tok = lambda w, c=0: pl.BlockSpec((1, TILE, w), lambda b, j: (b, j, c))
    dirs = lambda d: pl.BlockSpec((1, 1, TILE, MIX_W), lambda b, j: (d, b, j, 0))
    full = lambda r, c: pl.BlockSpec((r, c), lambda b, j: (0, 0))
    block_ones = np.kron(np.eye(N_HEADS), np.ones((HEAD_DIM, HEAD_DIM))).astype(np.float32)
    strict_low = np.tril(np.ones((TILE, TILE)), -1).astype(np.float32)
    strict_up = np.triu(np.ones((LANES, LANES)), 1).astype(np.float32)
    n_tiles = batch * nt
    return pl.pallas_call(
        _outproj_kernel,
        grid=(batch, nt),
        in_specs=[tok(D_MODEL), _mod_spec(batch),
                  dirs(0), dirs(1), tok(MIX_W, 4),
                  dirs(0), dirs(1), tok(MIX_W, 3),
                  tok(AT_HEADS * AT_HEAD_DIM),
                  full(1, MIX_W), full(1, MIX_W), full(MIX_W, MIX_W),
                  full(D_MODEL, D_MODEL), full(1, D_MODEL),
                  full(3 * D_MODEL, LANES), full(1, LANES), full(TILE, TILE), full(LANES, LANES)],
        out_specs=[tok(D_MODEL),
                   pl.BlockSpec((TOP_K * TILE, ROW_SUB, LANES), lambda b, j: (b * nt + j, 0, 0)),
                   tok(2 * TOP_K), tok(LANES),
                   pl.BlockSpec((1, 8, LANES), lambda b, j: (b * nt + j, 0, 0)),
                   pl.BlockSpec((1, LANES), lambda b, j: (0, 0))],
        out_shape=[jax.ShapeDtypeStruct((batch, rows, D_MODEL), F32),
                   jax.ShapeDtypeStruct((n_tiles * TOP_K * TILE, ROW_SUB, LANES), BF16),
                   jax.ShapeDtypeStruct((batch, rows, 2 * TOP_K), I32),
                   jax.ShapeDtypeStruct((batch, rows, LANES), F32),
                   jax.ShapeDtypeStruct((n_tiles, 8, LANES), I32),
                   jax.ShapeDtypeStruct((1, LANES), F32)],
        scratch_shapes=[pltpu.VMEM((1, LANES), F32)],
        compiler_params=_cparams(2, VMEM_LIMIT),
        name="outproj_router",
    )(xs, mod_l, o_hg, o_hg, p_hg, o_ml, o_ml, p_ml, att, hg_g, ml_g, jnp.asarray(block_ones, dtype=BF16),
      w_out_b, g2, w_router_p, b_router_p, jnp.asarray(strict_low, dtype=BF16),
      jnp.asarray(strict_up, dtype=BF16))


_RUN_BITS = tuple(1 << s for s in range(TILE.bit_length() - 1, -1, -1))


def _copy_run(src, dst, sem, n, src0, dst0, wait):
    for bit in _RUN_BITS:
        done = n - (n & (2 * bit - 1))

        @pl.when((n & bit) != 0)
        def _():
            cp = pltpu.make_async_copy(src.at[pl.ds(src0 + done, bit)], dst.at[pl.ds(dst0 + done, bit)], sem)
            cp.wait() if wait else cp.start()


def _dispatch_kernel(n_ref, a_ref, b_ref, zn_ref, zo_ref, src_ref, dst_ref, zero_scr, sem):
    t = pl.program_id(0)

    @pl.when(t == 0)
    def _():
        zero_scr[...] = jnp.zeros_like(zero_scr)

    for wait in (False, True):
        def per_expert(e, carry):
            i = t * N_EXPERTS + e
            _copy_run(src_ref, dst_ref, sem, n_ref[i], a_ref[i], b_ref[i], wait)
            return carry

        lax.fori_loop(0, N_EXPERTS, per_expert, 0)

        @pl.when(t == 0)
        def _():
            def per_gap(g, carry):
                def per_piece(i, c):
                    left = jnp.minimum(zn_ref[g] - i * TILE, TILE)
                    _copy_run(zero_scr, dst_ref, sem, left, 0, zo_ref[g] + i * TILE, wait)
                    return c

                lax.fori_loop(0, (zn_ref[g] + TILE - 1) // TILE, per_piece, 0)
                return carry

            lax.fori_loop(0, zn_ref.shape[0], per_gap, 0)


def _dispatch(run_n, run_a, run_b, fill, h_tiles, n_slots):
    n_tiles = run_n.shape[0] // N_EXPERTS
    grid_spec = pltpu.PrefetchScalarGridSpec(
        num_scalar_prefetch=5,
        grid=(n_tiles,),
        in_specs=[pl.BlockSpec((TOP_K * TILE, ROW_SUB, LANES), lambda t, *_: (t, 0, 0))],
        out_specs=pl.BlockSpec(memory_space=pl.ANY),
        scratch_shapes=[pltpu.VMEM((TILE, ROW_SUB, LANES), h_tiles.dtype), pltpu.SemaphoreType.DMA(())],
    )
    return pl.pallas_call(
        _dispatch_kernel,
        grid_spec=grid_spec,
        out_shape=jax.ShapeDtypeStruct((n_slots, ROW_SUB, LANES), h_tiles.dtype),
        compiler_params=_cparams(1, VMEM_LIMIT),
        name="moe_dispatch",
    )(run_n, run_a, run_b, fill[0], fill[1], h_tiles)


def _expert_kernel(be_ref, nv_ref, x_ref, wgu_ref, bgu_ref, wd_ref, bd_ref, y_ref, wgu_scr, wd_scr):
    i = pl.program_id(0)

    @pl.when(i < nv_ref[0])
    def _():
        prev = be_ref[jnp.maximum(i - 1, 0)]

        @pl.when((i == 0) | (prev != be_ref[i]))
        def _():
            wgu_scr[...] = wgu_ref[0, 0].astype(BF16)
            wd_scr[...] = wd_ref[0, 0].astype(BF16)

        x = x_ref[...].reshape(MOE_BLOCK, D_MODEL)
        gu = _dot(x, wgu_scr[...]) + bgu_ref[0, 0]
        gate = jnp.minimum(gu[:, 0:D_EXPERT], SWIGLU_LIMIT)
        up = jnp.clip(gu[:, D_EXPERT:], -SWIGLU_LIMIT, SWIGLU_LIMIT)
        act = (up + 1.0) * gate * (1.0 / (1.0 + jnp.exp(-SWIGLU_ALPHA * gate)))
        y = _dot(act.astype(BF16), wd_scr[...]) + bd_ref[0, 0]
        y_ref[...] = y.astype(BF16).reshape(MOE_BLOCK, ROW_SUB, LANES)

    @pl.when(i >= nv_ref[0])
    def _():
        y_ref[...] = jnp.zeros_like(y_ref)


def _experts(layer, block_e, n_valid, xs_sorted, w_gu, b_gu, w_down, b_down):
    n_slots = xs_sorted.shape[0]
    nb = n_slots // MOE_BLOCK
    depth = w_gu.shape[0]
    clamp = lambda i, nv: jnp.minimum(i, jnp.maximum(nv[0] - 1, 0))
    grid_spec = pltpu.PrefetchScalarGridSpec(
        num_scalar_prefetch=2,
        grid=(nb,),
        in_specs=[pl.BlockSpec((MOE_BLOCK, ROW_SUB, LANES), lambda i, be, nv: (clamp(i, nv), 0, 0)),
                  pl.BlockSpec((1, 1, D_MODEL, 2 * D_EXPERT), lambda i, be, nv: (layer, be[i], 0, 0)),
                  pl.BlockSpec((1, 1, 1, 2 * D_EXPERT), lambda i, be, nv: (layer, be[i], 0, 0)),
                  pl.BlockSpec((1, 1, D_EXPERT, D_MODEL), lambda i, be, nv: (layer, be[i], 0, 0)),
                  pl.BlockSpec((1, 1, 1, D_MODEL), lambda i, be, nv: (layer, be[i], 0, 0))],
        out_specs=pl.BlockSpec((MOE_BLOCK, ROW_SUB, LANES), lambda i, be, nv: (i, 0, 0)),
        scratch_shapes=[pltpu.VMEM((D_MODEL, 2 * D_EXPERT), BF16),
                        pltpu.VMEM((D_EXPERT, D_MODEL), BF16)],
    )
    return pl.pallas_call(
        _expert_kernel,
        grid_spec=grid_spec,
        out_shape=jax.ShapeDtypeStruct((n_slots, ROW_SUB, LANES), BF16),
        compiler_params=_cparams(1, VMEM_LIMIT),
        name="moe_experts",
    )(block_e, n_valid, xs_sorted, w_gu, b_gu.reshape(depth, N_EXPERTS, 1, 2 * D_EXPERT),
      w_down, b_down.reshape(depth, N_EXPERTS, 1, D_MODEL))


def _combine_kernel(n_ref, a_ref, b_ref, x_ref, mod_ref, gate_ref, pos_ref, y_ref, o_ref, buf, sem):
    t = pl.program_id(0) * pl.num_programs(1) + pl.program_id(1)
    n_tiles = pl.num_programs(0) * pl.num_programs(1)
    cur = t % 2

    def runs(tile, which, wait):
        def per_expert(e, carry):
            i = tile * N_EXPERTS + e
            _copy_run(y_ref, buf.at[which], sem.at[which], n_ref[i], b_ref[i], a_ref[i], wait)
            return carry

        lax.fori_loop(0, N_EXPERTS, per_expert, 0)

    @pl.when(t == 0)
    def _():
        runs(t, cur, False)

    @pl.when(t + 1 < n_tiles)
    def _():
        runs(t + 1, 1 - cur, False)

    runs(t, cur, True)

    gates = gate_ref[0]
    pos = pos_ref[0].astype(F32)
    slot_lane = lax.broadcasted_iota(I32, (TILE, TOP_K * TILE), 1).astype(F32)
    wmat = jnp.zeros((TILE, TOP_K * TILE), F32)
    for kk in range(TOP_K):
        wmat = wmat + jnp.where(slot_lane == pos[:, kk:kk + 1], gates[:, kk:kk + 1], 0.0)
    f = _dot(wmat.astype(BF16), buf[cur].reshape(TOP_K * TILE, D_MODEL))
    gate2 = mod_ref[0][:, 5 * D_MODEL:6 * D_MODEL]
    o_ref[0] = x_ref[0] + gate2 * f


def _combine(run_n, run_a, run_b, xs, mod_l, gates, pos, y_sorted):
    batch, rows, _ = xs.shape
    nt = rows // TILE
    tok = lambda w: pl.BlockSpec((1, TILE, w), lambda b, j, *_: (b, j, 0))
    mod_spec = _mod_spec(batch)
    grid_spec = pltpu.PrefetchScalarGridSpec(
        num_scalar_prefetch=3,
        grid=(batch, nt),
        in_specs=[tok(D_MODEL),
                  pl.BlockSpec(mod_spec.block_shape, lambda b, j, *_: mod_spec.index_map(b, j)),
                  tok(LANES), tok(2 * TOP_K), pl.BlockSpec(memory_space=pl.ANY)],
        out_specs=tok(D_MODEL),
        scratch_shapes=[pltpu.VMEM((2, TOP_K * TILE, ROW_SUB, LANES), y_sorted.dtype),
                        pltpu.SemaphoreType.DMA((2,))],
    )
    return pl.pallas_call(
        _combine_kernel,
        grid_spec=grid_spec,
        out_shape=jax.ShapeDtypeStruct((batch, rows, D_MODEL), F32),
        compiler_params=_cparams(2, VMEM_LIMIT),
        name="moe_combine",
    )(run_n, run_a, run_b, xs, mod_l, gates, pos, y_sorted)


def _moe_plan(counts, tables, n_tokens):
    cnt = counts[0, :N_EXPERTS].astype(I32)
    padded = (cnt + MOE_BLOCK - 1) // MOE_BLOCK * MOE_BLOCK
    pend = jnp.cumsum(padded)
    pstart = pend - padded
    nb = -(-(n_tokens * TOP_K) // MOE_BLOCK) + N_EXPERTS
    block_row = jnp.arange(nb, dtype=I32) * MOE_BLOCK
    block_e = jnp.minimum(jnp.sum((pend[None, :] <= block_row[:, None]).astype(I32), axis=1), N_EXPERTS - 1)
    n_valid = (pend[-1] // MOE_BLOCK).astype(I32).reshape(1)
    run_n = tables[:, 0, :N_EXPERTS]
    tile_pos = tables[:, 1, :N_EXPERTS]
    slot_pos = tables[:, 2, :N_EXPERTS] + pstart[None, :]
    fill_n = jnp.concatenate([padded - cnt, (nb * MOE_BLOCK - pend[-1])[None]])
    fill_at = jnp.concatenate([pstart + cnt, pend[-1:]])
    flat = lambda a: a.reshape(-1).astype(I32)
    return (flat(run_n), flat(tile_pos), flat(slot_pos), (flat(fill_n), flat(fill_at)),
            block_e.astype(I32), n_valid, nb * MOE_BLOCK)


def _arrange_w_in(w_in_l):
    hg = w_in_l[:, 0:HG_IN]
    ml0 = HG_IN
    ml = w_in_l[:, ml0:ml0 + ML_IN]
    gt = w_in_l[:, ml0 + ML_IN:ml0 + ML_IN + 4 * N_HEADS]
    at = w_in_l[:, ml0 + ML_IN + 4 * N_HEADS:]
    gt = jnp.repeat(_gate_order(gt.T).T, HEAD_DIM, axis=1)
    return jnp.concatenate([hg, ml, gt, at], 1).astype(BF16)


def _gate_order(g):
    g = g.reshape((4, N_HEADS) + g.shape[1:]) if g.shape[0] == 4 * N_HEADS else g
    return jnp.concatenate([g[0], g[2], g[1], g[3]], axis=0)


def _hgrn2_lower_bounds(lb_logits):
    p = jax.nn.softmax(lb_logits.astype(F32), axis=1)
    cum = jnp.cumsum(p, axis=1)
    return cum - cum[:, :1]


def kernel(x, c, ctx, c_ctx, w_mod, b_mod, norm1_g, norm2_g, w_in, w_out, hg_lb_logits, hg_norm_g,
           ml_gate_bias, ml_norm_g, q_norm_g, k_norm_g, w_router, b_router, w_gu, b_gu, w_down, b_down):
    batch, seq, d = x.shape
    ctx_len = ctx.shape[1]
    depth = w_mod.shape[0]
    assert d == D_MODEL and ctx_len == TILE and seq % TILE == 0 and batch + 1 <= MOD_ROWS
    rows = ctx_len + seq
    n_tokens = batch * rows

    cvec = jnp.concatenate([c, c_ctx[None, :], jnp.zeros((MOD_ROWS - batch - 1, d), F32)], 0)
    mod = _modulation(cvec, w_mod, b_mod).reshape(depth, MOD_ROWS, 1, N_MOD)
    hg_lb = _hgrn2_lower_bounds(hg_lb_logits)
    cos, sin = _rope_tables(rows, ctx_len)
    xs = jnp.concatenate([ctx, x], axis=1)

    for layer in range(depth):
        mod_l = mod[layer]
        p_hg, p_ml, p_gt, qn, kn, vb = _inproj(xs, mod_l, norm1_g[layer][None, :], _arrange_w_in(w_in[layer]),
                                               cos, sin, q_norm_g[layer][None, :], k_norm_g[layer][None, :])
        o_hg = _hgrn2(p_hg, hg_lb[:, layer])
        gb = jnp.repeat(_gate_order(ml_gate_bias[layer]), HEAD_DIM).reshape(4, 1, MIX_W)
        o_ml = _mlstm(p_ml, p_gt, gb[0::2], gb[1::2])
        att = _attention(qn, kn, vb)
        w_router_p = jnp.pad(w_router[layer], ((0, 0), (0, LANES - N_EXPERTS)))
        wr_hi = w_router_p.astype(BF16)
        wr_lo = (w_router_p - wr_hi.astype(F32)).astype(BF16)
        w_router_p = jnp.concatenate([wr_hi, wr_hi, wr_lo], axis=0)
        b_router_p = jnp.pad(b_router[layer], (0, LANES - N_EXPERTS))[None, :]
        xs, h_tiles, pos, gates, tables, counts = _outproj(
            xs, mod_l, o_hg, p_hg, o_ml, p_ml, att,
            jnp.tile(hg_norm_g[layer], N_HEADS)[None, :], jnp.tile(ml_norm_g[layer], N_HEADS)[None, :],
            w_out[layer].astype(BF16), norm2_g[layer][None, :], w_router_p, b_router_p)
        run_n, tile_pos, slot_pos, fill, block_e, n_valid, n_slots = _moe_plan(counts, tables, n_tokens)
        xs_sorted = _dispatch(run_n, tile_pos, slot_pos, fill, h_tiles, n_slots)
        y_sorted = _experts(layer, block_e, n_valid, xs_sorted, w_gu, b_gu, w_down, b_down)
        xs = _combine(run_n, tile_pos, slot_pos, xs, mod_l, gates, pos, y_sorted)
    return xs[:, ctx_len:, :]
```

```python
import functools

import numpy as np
import jax
import jax.numpy as jnp
from jax import lax
from jax.experimental import pallas as pl
from jax.experimental.pallas import tpu as pltpu

F32 = jnp.float32
BF16 = jnp.bfloat16
I32 = jnp.int32
HIGHEST = lax.Precision.HIGHEST

D_MODEL = 1024
NORM_EPS = 1e-6
CHUNK = 64
TILE = 256
CHUNKS_PER_TILE = TILE // CHUNK
SCAN_PAIR = 2
GRID_W = 64
ROPE_THETA = 10000.0

N_HEADS = 4
HEAD_DIM = 64
MIX_W = N_HEADS * HEAD_DIM
AT_HEADS = 4
AT_KV_HEADS = 2
AT_HEAD_DIM = 128
AT_GROUP = AT_HEADS // AT_KV_HEADS
ATT_KEY_CHUNK = 2176

HG_IN = 5 * MIX_W
ML_IN = 4 * MIX_W
AT_IN = (AT_HEADS + 2 * AT_KV_HEADS) * AT_HEAD_DIM
GATE_W = 4 * MIX_W

N_EXPERTS = 32
TOP_K = 4
D_EXPERT = 1024
SWIGLU_LIMIT = 7.0
SWIGLU_ALPHA = 1.702
MOE_BLOCK = 512
ROW_SUB = D_MODEL // 128
LANES = 128
N_MOD = 6 * D_MODEL
MOD_ROWS = 16

VMEM_LIMIT = 56 * 1024 * 1024


def _cparams(n_axes, vmem=None):
    kw = dict(dimension_semantics=("arbitrary",) * n_axes)
    if vmem is not None:
        kw["vmem_limit_bytes"] = vmem
    return pltpu.CompilerParams(**kw)


def _dot(a, b, precision=None):
    return jnp.dot(a, b, preferred_element_type=F32, precision=precision)


def _dot_nt(a, b):
    return lax.dot_general(a, b, (((1,), (1,)), ((), ())), preferred_element_type=F32)


def _dot_tn(a, b, precision=None):
    return lax.dot_general(a, b, (((0,), (0,)), ((), ())), preferred_element_type=F32,
                           precision=precision)


def _log_sigmoid(z):
    return jnp.minimum(z, 0.0) - jnp.log1p(jnp.exp(-jnp.abs(z)))


def _rms(x, eps=NORM_EPS):
    return x * lax.rsqrt(jnp.mean(x * x, axis=-1, keepdims=True) + eps)


def _mod_kernel(c_ref, w_ref, b_ref, o_ref):
    cv = c_ref[...]
    s = cv * (1.0 / (1.0 + jnp.exp(-cv)))
    o_ref[0] = _dot(s, w_ref[0], precision=HIGHEST) + b_ref[0]


def _modulation(cvec, w_mod, b_mod):
    depth = w_mod.shape[0]
    nblk = 1536
    return pl.pallas_call(
        _mod_kernel,
        grid=(depth, N_MOD // nblk),
        in_specs=[pl.BlockSpec((MOD_ROWS, D_MODEL), lambda l, n: (0, 0)),
                  pl.BlockSpec((1, D_MODEL, nblk), lambda l, n: (l, 0, n)),
                  pl.BlockSpec((1, 1, nblk), lambda l, n: (l, 0, n))],
        out_specs=pl.BlockSpec((1, MOD_ROWS, nblk), lambda l, n: (l, 0, n)),
        out_shape=jax.ShapeDtypeStruct((depth, MOD_ROWS, N_MOD), F32),
        compiler_params=_cparams(2, VMEM_LIMIT),
        name="modulation",
    )(cvec, w_mod, b_mod.reshape(depth, 1, N_MOD))


def _mod_spec(batch):
    return pl.BlockSpec((1, 1, N_MOD), lambda b, j: (jnp.where(j == 0, batch, b), 0, 0))


def _inproj_kernel(x_ref, mod_ref, g_ref, w_ref, cos_ref, sin_ref, qg_ref, kg_ref,
                   hg_ref, ml_ref, gt_ref, q_ref, k_ref, v_ref):
    x = x_ref[0]
    mod = mod_ref[0]
    sh = mod[:, 0:D_MODEL]
    sc = mod[:, D_MODEL:2 * D_MODEL]
    h = _rms(x) * g_ref[...]
    h = (h * (1.0 + sc) + sh).astype(BF16)
    o0 = 0
    for ref, width in ((hg_ref, HG_IN), (ml_ref, ML_IN), (gt_ref, GATE_W)):
        ref[0] = _dot(h, w_ref[:, o0:o0 + width])
        o0 += width

    at = _dot(h, w_ref[:, o0:o0 + AT_IN])
    cos = cos_ref[...]
    sin = sin_ref[...]
    lane = lax.broadcasted_iota(I32, (TILE, AT_HEAD_DIM), 1)
    quarter = AT_HEAD_DIM // 4
    first_half = (lane % (2 * quarter)) < quarter

    def norm_rope(y, g):
        y = _rms(y) * g
        partner = jnp.where(first_half, pltpu.roll(y, AT_HEAD_DIM - quarter, axis=1), pltpu.roll(y, quarter, axis=1))
        return y * cos + partner * sin

    head = lambda i: at[:, i * AT_HEAD_DIM:(i + 1) * AT_HEAD_DIM]
    for i in range(AT_HEADS):
        q_ref[0, :, i * AT_HEAD_DIM:(i + 1) * AT_HEAD_DIM] = (
            norm_rope(head(i), qg_ref[...]) * (AT_HEAD_DIM ** -0.5)).astype(BF16)
    for i in range(AT_KV_HEADS):
        k_ref[0, :, i * AT_HEAD_DIM:(i + 1) * AT_HEAD_DIM] = norm_rope(head(AT_HEADS + i), kg_ref[...]).astype(BF16)
        v_ref[0, :, 2 * i * AT_HEAD_DIM:(2 * i + 1) * AT_HEAD_DIM] = head(AT_HEADS + AT_KV_HEADS + i).astype(BF16)
        v_ref[0, :, (2 * i + 1) * AT_HEAD_DIM:(2 * i + 2) * AT_HEAD_DIM] = jnp.ones((TILE, AT_HEAD_DIM), BF16)


def _inproj(xs, mod_l, g1, w_in_r, cos, sin, q_g, k_g):
    batch, rows, _ = xs.shape
    nt = rows // TILE
    n_all = HG_IN + ML_IN + GATE_W + AT_IN
    tok = lambda w: pl.BlockSpec((1, TILE, w), lambda b, j: (b, j, 0))
    tab = pl.BlockSpec((TILE, AT_HEAD_DIM), lambda b, j: (j, 0))
    vec = pl.BlockSpec((1, AT_HEAD_DIM), lambda b, j: (0, 0))
    kvw = AT_KV_HEADS * AT_HEAD_DIM
    widths = ((HG_IN, F32), (ML_IN, F32), (GATE_W, F32), (AT_HEADS * AT_HEAD_DIM, BF16), (kvw, BF16), (2 * kvw, BF16))
    return pl.pallas_call(
        _inproj_kernel,
        grid=(batch, nt),
        in_specs=[tok(D_MODEL), _mod_spec(batch),
                  pl.BlockSpec((1, D_MODEL), lambda b, j: (0, 0)),
                  pl.BlockSpec((D_MODEL, n_all), lambda b, j: (0, 0)),
                  tab, tab, vec, vec],
        out_specs=[tok(w) for w, _ in widths],
        out_shape=[jax.ShapeDtypeStruct((batch, rows, w), dt) for w, dt in widths],
        compiler_params=_cparams(2, VMEM_LIMIT),
        name="inproj",
    )(xs, mod_l, g1, w_in_r, cos, sin, q_g, k_g)


def _tile_of(d, j, nt):
    return j + d * jnp.where(j > 0, nt - 2 * j, 0)


def _chunk_of(d, i):
    return i + d * (CHUNKS_PER_TILE - 1 - 2 * i)


_HG_LEVELS = (32, 16, 8, 4, 2, 1)


def _head_block_mask():
    h = np.arange(MIX_W) // HEAD_DIM
    return (h[:, None] == h[None, :]).astype(np.float32)


def _split_hi_lo(x):
    hi = x.astype(BF16)
    lo = (x - hi.astype(F32)).astype(BF16)
    return jnp.concatenate([hi, lo], axis=0)


def _block_diag_rows(x, lane_head):
    zero = jnp.zeros_like(x)
    return jnp.concatenate([jnp.where(lane_head == h, x, zero) for h in range(N_HEADS)], axis=0)


@functools.lru_cache(maxsize=None)
def _hgrn2_consts():
    c = CHUNK
    tri = np.tril(np.ones((c, c)))
    ops = [tri]
    masks = []
    pos = np.arange(c)
    for m in _HG_LEVELS:
        blk = pos // (2 * m)
        sel = np.zeros((c, c))
        sel[pos, 2 * m * blk + m - 1] = 1.0
        ops.append(tri - sel @ tri)
        in_b = (pos % (2 * m)) >= m
        masks.append(((blk[:, None] == blk[None, :]) & in_b[:, None] & ~in_b[None, :]).astype(np.float64))
    masks.append(np.eye(c))
    ops.append(np.ones((c, c)) - tri)
    flip = np.eye(c)[::-1]
    stack = lambda xs: np.concatenate(xs, 0)
    ops_d = np.stack([stack(ops), stack([flip @ o @ flip for o in ops])])
    ops_d = np.concatenate([ops_d, ops_d], axis=2)
    tile = lambda mk: np.tile(mk, (1, N_HEADS))
    masks_d = np.stack([np.stack([tile(mk) for mk in masks]), np.stack([tile(mk[::-1, ::-1]) for mk in masks])])
    return ops_d.astype(np.float32), masks_d.astype(np.float32)


def _hgrn2_kernel(q_ref, z_ref, v_ref, lb_ref, ops_ref, msk_ref, hm_ref, o_ref, st_scr, lf_scr, k_scr, *,
                  direction):
    d = direction
    j = pl.program_id(1)

    @pl.when(j == 0)
    def _():
        st_scr[...] = jnp.zeros_like(st_scr)

    lb = lb_ref[0]
    z = z_ref[...]
    a = jnp.log(lb)
    cterm = jnp.log1p(-lb) + _log_sigmoid(z)
    lf_scr[...] = jnp.maximum(a, cterm) + jnp.log1p(jnp.exp(-jnp.abs(a - cterm)))
    k_scr[...] = (1.0 - lb) / (1.0 + jnp.exp(z))
    n_lv = len(_HG_LEVELS)
    lane_head = lax.broadcasted_iota(I32, (CHUNK, MIX_W), 1) // HEAD_DIM

    for i, bb in [(i, bb) for i in range(CHUNKS_PER_TILE) for bb in range(SCAN_PAIR)]:
        rows = pl.ds(_chunk_of(d, i) * CHUNK, CHUNK)
        cs = _dot(ops_ref[0], _split_hi_lo(lf_scr[bb, rows, :]))
        b = cs[0:CHUNK]
        rest = cs[(n_lv + 1) * CHUNK:(n_lv + 2) * CHUNK]
        q = q_ref[bb, rows, :]
        k = k_scr[bb, rows, :]
        vb = v_ref[bb, rows, :].astype(BF16)
        q_in = (q * jnp.exp(b)).astype(BF16)
        k_out = (k * jnp.exp(rest)).astype(BF16)
        s_decay = jnp.exp(b[0:1] + rest[0:1])
        sc = jnp.zeros((CHUNK, MIX_W), F32)
        for lv in range(n_lv + 1):
            if lv < n_lv:
                e = jnp.exp(-jnp.abs(cs[(lv + 1) * CHUNK:(lv + 2) * CHUNK]))
                ql = (q * e).astype(BF16)
                kl = (k * e).astype(BF16)
            else:
                ql = q.astype(BF16)
                kl = k.astype(BF16)
            sc = sc + msk_ref[0, lv] * _dot_nt(ql, _block_diag_rows(kl, lane_head))
        st = st_scr[bb]
        o_ref[bb, rows, :] = (_dot(sc.astype(BF16), _block_diag_rows(vb, lane_head))
                              + _dot_nt(q_in, st.astype(BF16)))
        st_scr[bb] = st * s_decay + hm_ref[...] * _dot_tn(vb, k_out)


def _hgrn2(p_hg, lb_l):
    batch, rows, _ = p_hg.shape
    nt = rows // TILE
    ops_d, masks_d = _hgrn2_consts()
    n_ops = ops_d.shape[1]
    n_msk = masks_d.shape[1]
    def one_direction(d):
        col = lambda c: pl.BlockSpec((SCAN_PAIR, TILE, MIX_W), lambda b, j: (b, _tile_of(d, j, nt), c))
        const = lambda shp: pl.BlockSpec((1,) + shp, lambda b, j: (d,) + (0,) * len(shp))
        return pl.pallas_call(
            functools.partial(_hgrn2_kernel, direction=d),
            grid=(batch // SCAN_PAIR, nt),
            in_specs=[col(0), col(1 + d), col(3),
                      const((1, MIX_W)), const((n_ops, 2 * CHUNK)), const((n_msk, CHUNK, MIX_W)),
                      pl.BlockSpec((MIX_W, MIX_W), lambda b, j: (0, 0))],
            out_specs=col(0),
            out_shape=jax.ShapeDtypeStruct((batch, rows, MIX_W), F32),
            scratch_shapes=[pltpu.VMEM((SCAN_PAIR, MIX_W, MIX_W), F32),
                            pltpu.VMEM((SCAN_PAIR, TILE, MIX_W), F32),
                            pltpu.VMEM((SCAN_PAIR, TILE, MIX_W), F32)],
            compiler_params=_cparams(2, VMEM_LIMIT),
            name="hgrn2",
        )(p_hg, p_hg, p_hg, lb_l.reshape(2, 1, MIX_W), jnp.asarray(ops_d, dtype=BF16), jnp.asarray(masks_d),
          jnp.asarray(_head_block_mask()))

    return one_direction(0), one_direction(1)


@functools.lru_cache(maxsize=None)
def _mlstm_consts():
    c = CHUNK
    tri = np.tril(np.ones((c, c)))
    flip = np.eye(c)[::-1]
    tile = lambda mk: np.tile(mk, (1, N_HEADS))
    ops, masks = [], []
    for t in (tri, flip @ tri @ flip):
        ops.append(np.concatenate([t, t], axis=1))
        masks.append(np.stack([tile(t.T), tile(t), tile(np.eye(c))]))
    return np.stack(ops).astype(np.float32), np.stack(masks).astype(np.float32)


def _mlstm_kernel(q_ref, k_ref, v_ref, gi_ref, gf_ref, bi_ref, bf_ref, op_ref, cm_ref, hm_ref, h_ref,
                  ct_scr, n_scr, m_scr, i_scr, f_scr, *, direction):
    d = direction
    j = pl.program_id(1)

    @pl.when(j == 0)
    def _():
        ct_scr[...] = jnp.zeros_like(ct_scr)
        n_scr[...] = jnp.zeros_like(n_scr)
        m_scr[...] = jnp.zeros_like(m_scr)

    i_scr[...] = gi_ref[...] + bi_ref[0]
    f_scr[...] = _log_sigmoid(gf_ref[...] + bf_ref[0])
    lane_head = lax.broadcasted_iota(I32, (CHUNK, MIX_W), 1) // HEAD_DIM
    neg_inf = F32(-jnp.inf)
    hm = hm_ref[...]
    hm_b = hm.astype(BF16)

    for i, bb in [(i, bb) for i in range(CHUNKS_PER_TILE) for bb in range(SCAN_PAIR)]:
        rows = pl.ds(_chunk_of(d, i) * CHUNK, CHUNK)
        ie = i_scr[bb, rows, :]
        fe = f_scr[bb, rows, :]
        bc = _dot(op_ref[0], _split_hi_lo(fe))
        row = jnp.sum(ie * cm_ref[0, 2] - fe * cm_ref[0, 0], axis=0, keepdims=True)
        tot = jnp.sum(fe, axis=0, keepdims=True)
        m_st = m_scr[bb]
        dm = jnp.where(cm_ref[0, 1] > 0.5, bc + row, neg_inf)
        m_in = jnp.zeros((CHUNK, MIX_W), F32)
        for h in range(N_HEADS):
            mh = jnp.max(dm[:, h * HEAD_DIM:(h + 1) * HEAD_DIM], axis=-1, keepdims=True)
            m_in = jnp.where(lane_head == h, mh, m_in)
        from_prev = bc + m_st
        m_t = jnp.maximum(from_prev, m_in)
        q = q_ref[bb, rows, :]
        k = k_ref[bb, rows, :] * (HEAD_DIM ** -0.5)
        qb = q.astype(BF16)
        vb = v_ref[bb, rows, :].astype(BF16)
        w = jnp.exp(dm - m_t) * _dot_nt(qb, _block_diag_rows(k.astype(BF16), lane_head))
        a_prev = jnp.exp(from_prev - m_t)
        ct = ct_scr[bb]
        n_st = n_scr[bb]
        num = _dot(w.astype(BF16), _block_diag_rows(vb, lane_head)) + a_prev * _dot_nt(qb, ct.astype(BF16))
        den = _dot((w + a_prev * (q * n_st)).astype(BF16), hm_b)
        h_ref[bb, rows, :] = num / jnp.maximum(jnp.abs(den), jnp.exp(-m_t))
        e_col = tot - bc + ie
        m_new = jnp.maximum(tot + m_st, jnp.max(e_col, axis=0, keepdims=True))
        a_end = jnp.exp(tot + m_st - m_new)
        kw = k * jnp.exp(e_col - m_new)
        ct_scr[bb] = ct * a_end + hm * _dot_tn(vb, kw.astype(BF16))
        n_scr[bb] = a_end * n_st + jnp.sum(kw, axis=0, keepdims=True)
        m_scr[bb] = m_new


def _mlstm(p_ml, p_gt, bias_i, bias_f):
    batch, rows, _ = p_ml.shape
    nt = rows // TILE
    ops_d, masks_d = _mlstm_consts()
    def one_direction(d):
        col = lambda c: pl.BlockSpec((SCAN_PAIR, TILE, MIX_W), lambda b, j: (b, _tile_of(d, j, nt), c))
        const = lambda shp: pl.BlockSpec((1,) + shp, lambda b, j: (d,) + (0,) * len(shp))
        return pl.pallas_call(
            functools.partial(_mlstm_kernel, direction=d),
            grid=(batch // SCAN_PAIR, nt),
            in_specs=[col(0), col(1), col(2), col(2 * d), col(2 * d + 1),
                      const((1, MIX_W)), const((1, MIX_W)),
                      const((CHUNK, 2 * CHUNK)), const((3, CHUNK, MIX_W)),
                      pl.BlockSpec((MIX_W, MIX_W), lambda b, j: (0, 0))],
            out_specs=col(0),
            out_shape=jax.ShapeDtypeStruct((batch, rows, MIX_W), F32),
            scratch_shapes=[pltpu.VMEM((SCAN_PAIR, MIX_W, MIX_W), F32),
                            pltpu.VMEM((SCAN_PAIR, 1, MIX_W), F32),
                            pltpu.VMEM((SCAN_PAIR, 1, MIX_W), F32),
                            pltpu.VMEM((SCAN_PAIR, TILE, MIX_W), F32),
                            pltpu.VMEM((SCAN_PAIR, TILE, MIX_W), F32)],
            compiler_params=_cparams(2, VMEM_LIMIT),
            name="mlstm",
        )(p_ml, p_ml, p_ml, p_gt, p_gt, bias_i, bias_f, jnp.asarray(ops_d, dtype=BF16), jnp.asarray(masks_d),
          jnp.asarray(_head_block_mask()))

    return one_direction(0), one_direction(1)


def _rope_tables(rows, ctx_len):
    n = rows - ctx_len
    r = jnp.arange(n) // GRID_W
    cc = jnp.arange(n) % GRID_W
    axis_dims = AT_HEAD_DIM // 2
    inv_freq = ROPE_THETA ** (-jnp.arange(0, axis_dims, 2, dtype=F32) / axis_dims)
    ang_r = r.astype(F32)[:, None] * inv_freq
    ang_c = cc.astype(F32)[:, None] * inv_freq
    cos = jnp.concatenate([jnp.cos(ang_r), jnp.cos(ang_r), jnp.cos(ang_c), jnp.cos(ang_c)], -1)
    sin = jnp.concatenate([-jnp.sin(ang_r), jnp.sin(ang_r), -jnp.sin(ang_c), jnp.sin(ang_c)], -1)
    cos = jnp.concatenate([jnp.ones((ctx_len, AT_HEAD_DIM), F32), cos], 0)
    sin = jnp.concatenate([jnp.zeros((ctx_len, AT_HEAD_DIM), F32), sin], 0)
    return cos, sin


def _attn_kernel(q_ref, k_ref, v_ref, o_ref):
    j = pl.program_id(2)

    def attend(n_keys):
        q2 = jnp.concatenate([q_ref[0, :, g * AT_HEAD_DIM:(g + 1) * AT_HEAD_DIM] for g in range(AT_GROUP)], axis=0)
        m = jnp.full((AT_GROUP * TILE, 1), -jnp.inf, F32)
        acc = jnp.zeros((AT_GROUP * TILE, 2 * AT_HEAD_DIM), F32)
        bounds = list(range(0, n_keys, ATT_KEY_CHUNK))
        bounds = bounds[:max(len(bounds) - 1, 1)] if n_keys % ATT_KEY_CHUNK else bounds
        bounds.append(n_keys)
        for c0, c1 in zip(bounds[:-1], bounds[1:]):
            s = _dot_nt(q2, k_ref[0, c0:c1, :])
            m_new = jnp.maximum(m, jnp.max(s, axis=-1, keepdims=True))
            p = jnp.exp((s - m_new).astype(BF16))
            acc = acc * jnp.exp(m - m_new) + _dot(p, v_ref[0, c0:c1, :])
            m = m_new
        o = acc[:, 0:AT_HEAD_DIM] / acc[:, AT_HEAD_DIM:AT_HEAD_DIM + 1]
        for g in range(AT_GROUP):
            o_ref[0, :, g * AT_HEAD_DIM:(g + 1) * AT_HEAD_DIM] = o[g * TILE:(g + 1) * TILE]

    @pl.when(j == 0)
    def _():
        attend(TILE)

    @pl.when(j > 0)
    def _():
        attend(k_ref.shape[1])


def _attention(qn, kn, vb):
    batch, rows, _ = qn.shape
    nt = rows // TILE
    gw = AT_GROUP * AT_HEAD_DIM
    return pl.pallas_call(
        _attn_kernel,
        grid=(batch, AT_KV_HEADS, nt),
        in_specs=[pl.BlockSpec((1, TILE, gw), lambda b, h, j: (b, j, h)),
                  pl.BlockSpec((1, rows, AT_HEAD_DIM), lambda b, h, j: (b, 0, h)),
                  pl.BlockSpec((1, rows, 2 * AT_HEAD_DIM), lambda b, h, j: (b, 0, h))],
        out_specs=pl.BlockSpec((1, TILE, gw), lambda b, h, j: (b, j, h)),
        out_shape=jax.ShapeDtypeStruct((batch, rows, AT_HEADS * AT_HEAD_DIM), F32),
        compiler_params=_cparams(3, VMEM_LIMIT),
        name="attention",
    )(qn, kn, vb)


def _outproj_kernel(x_ref, mod_ref, ohf_ref, ohb_ref, gh_ref, omf_ref, omb_ref, gm_ref, at_ref,
                    hgg_ref, mlg_ref, ones_ref, wo_ref, g2_ref, wr_ref, br_ref, low_ref, up_ref,
                    xo_ref, st_ref, pos_ref, gate_ref, tab_ref, cnt_ref, cnt_scr):
    b = pl.program_id(0)
    j = pl.program_id(1)

    @pl.when((b == 0) & (j == 0))
    def _():
        cnt_scr[...] = jnp.zeros_like(cnt_scr)

    mod = mod_ref[0]
    gate1 = mod[:, 2 * D_MODEL:3 * D_MODEL]
    sh2 = mod[:, 3 * D_MODEL:4 * D_MODEL]
    sc2 = mod[:, 4 * D_MODEL:5 * D_MODEL]

    def head_norm(o, g):
        ms = _dot((o * o).astype(BF16), ones_ref[...]) * (1.0 / HEAD_DIM)
        return o * lax.rsqrt(ms + NORM_EPS) * g

    gh = gh_ref[0]
    y_hg = head_norm(ohf_ref[0] + ohb_ref[0], hgg_ref[...]) * (gh / (1.0 + jnp.exp(-gh)))
    y_ml = head_norm(omf_ref[0] + omb_ref[0], mlg_ref[...]) / (1.0 + jnp.exp(-gm_ref[0]))
    proj = (_dot(y_hg.astype(BF16), wo_ref[0:MIX_W, :])
            + _dot(y_ml.astype(BF16), wo_ref[MIX_W:2 * MIX_W, :])
            + _dot(at_ref[0].astype(BF16), wo_ref[2 * MIX_W:, :]))
    x = x_ref[0] + gate1 * proj
    xo_ref[0] = x
    h2 = _rms(x) * g2_ref[...]
    h2 = h2 * (1.0 + sc2) + sh2

    lane = lax.broadcasted_iota(I32, (TILE, LANES), 1)
    lane_f = lane.astype(F32)
    h_hi = h2.astype(BF16)
    h_lo = (h2 - h_hi.astype(F32)).astype(BF16)
    logits = _dot(jnp.concatenate([h_hi, h_lo, h_hi], axis=1), wr_ref[...]) + br_ref[...]
    lg = jnp.where(lane < N_EXPERTS, logits, -jnp.inf)
    multi = jnp.zeros((TILE, LANES), F32)
    hots, vals = [], []
    for _ in range(TOP_K):
        mx = jnp.max(lg, axis=-1, keepdims=True)
        idx = jnp.min(jnp.where(lg == mx, lane_f, F32(LANES)), axis=-1, keepdims=True)
        hot = lane_f == idx
        lg = jnp.where(hot, -jnp.inf, lg)
        multi = multi + hot.astype(F32)
        hots.append(hot)
        vals.append(mx)
    es = [jnp.exp(v - vals[0]) for v in vals]
    den = es[0] + es[1] + es[2] + es[3]

    tile_cnt = jnp.sum(multi, axis=0, keepdims=True)
    tile_start = _dot(jnp.broadcast_to(tile_cnt, (8, LANES)).astype(BF16), up_ref[...])[0:1]
    where_in_tile = _dot(low_ref[...], multi.astype(BF16)) + tile_start
    slot_lane = lax.broadcasted_iota(I32, (TILE, TOP_K * TILE), 1).astype(F32)
    onehot = jnp.zeros((TILE, TOP_K * TILE), F32)
    pos_out = jnp.zeros((TILE, LANES), I32)
    gate_out = jnp.zeros((TILE, LANES), F32)
    for kk in range(TOP_K):
        pos = jnp.sum(jnp.where(hots[kk], where_in_tile, 0.0), axis=-1, keepdims=True)
        onehot = onehot + jnp.where(slot_lane == pos, 1.0, 0.0)
        pos_out = pos_out + jnp.where(lane == kk, pos.astype(I32), 0)
        gate_out = gate_out + jnp.where(lane == kk, es[kk] / den, 0.0)
    sorted_rows = _dot_tn(onehot.astype(BF16), h_hi)
    st_ref[...] = sorted_rows.astype(BF16).reshape(TOP_K * TILE, ROW_SUB, LANES)
    pos_ref[0] = pos_out[:, 0:2 * TOP_K]
    gate_ref[0] = gate_out
    tab_ref[0] = jnp.concatenate([tile_cnt, tile_start, cnt_scr[...], jnp.zeros((5, LANES), F32)], axis=0).astype(I32)
    cnt_scr[...] = cnt_scr[...] + tile_cnt
    cnt_ref[...] = cnt_scr[...]


def _outproj(xs, mod_l, o_hg, p_hg, o_ml, p_ml, att, hg_g, ml_g, w_out_b, g2, w_router_p, b_router_p):
    batch, rows, _ = xs.shape
    nt = rows // TILE
    tok = lambda w, c=0: pl.BlockSpec((1, TILE, w), lambda b, j: (b, j, c))
    dirs = lambda d: tok(MIX_W)
    full = lambda r, c: pl.BlockSpec((r, c), lambda b, j: (0, 0))
    block_ones = np.kron(np.eye(N_HEADS), np.ones((HEAD_DIM, HEAD_DIM))).astype(np.float32)
    strict_low = np.tril(np.ones((TILE, TILE)), -1).astype(np.float32)
    strict_up = np.triu(np.ones((LANES, LANES)), 1).astype(np.float32)
    n_tiles = batch * nt
    return pl.pallas_call(
        _outproj_kernel,
        grid=(batch, nt),
        in_specs=[tok(D_MODEL), _mod_spec(batch),
                  dirs(0), dirs(1), tok(MIX_W, 4),
                  dirs(0), dirs(1), tok(MIX_W, 3),
                  tok(AT_HEADS * AT_HEAD_DIM),
                  full(1, MIX_W), full(1, MIX_W), full(MIX_W, MIX_W),
                  full(D_MODEL, D_MODEL), full(1, D_MODEL),
                  full(3 * D_MODEL, LANES), full(1, LANES), full(TILE, TILE), full(LANES, LANES)],
        out_specs=[tok(D_MODEL),
                   pl.BlockSpec((TOP_K * TILE, ROW_SUB, LANES), lambda b, j: (b * nt + j, 0, 0)),
                   tok(2 * TOP_K), tok(LANES),
                   pl.BlockSpec((1, 8, LANES), lambda b, j: (b * nt + j, 0, 0)),
                   pl.BlockSpec((1, LANES), lambda b, j: (0, 0))],
        out_shape=[jax.ShapeDtypeStruct((batch, rows, D_MODEL), F32),
                   jax.ShapeDtypeStruct((n_tiles * TOP_K * TILE, ROW_SUB, LANES), BF16),
                   jax.ShapeDtypeStruct((batch, rows, 2 * TOP_K), I32),
                   jax.ShapeDtypeStruct((batch, rows, LANES), F32),
                   jax.ShapeDtypeStruct((n_tiles, 8, LANES), I32),
                   jax.ShapeDtypeStruct((1, LANES), F32)],
        scratch_shapes=[pltpu.VMEM((1, LANES), F32)],
        compiler_params=_cparams(2, VMEM_LIMIT),
        name="outproj_router",
    )(xs, mod_l, o_hg[0], o_hg[1], p_hg, o_ml[0], o_ml[1], p_ml, att, hg_g, ml_g,
      jnp.asarray(block_ones, dtype=BF16),
      w_out_b, g2, w_router_p, b_router_p, jnp.asarray(strict_low, dtype=BF16),
      jnp.asarray(strict_up, dtype=BF16))


_RUN_BITS = tuple(1 << s for s in range(TILE.bit_length() - 1, -1, -1))


def _copy_run(src, dst, sem, n, src0, dst0, wait):
    for bit in _RUN_BITS:
        done = n - (n & (2 * bit - 1))

        @pl.when((n & bit) != 0)
        def _():
            cp = pltpu.make_async_copy(src.at[pl.ds(src0 + done, bit)], dst.at[pl.ds(dst0 + done, bit)], sem)
            cp.wait() if wait else cp.start()


def _dispatch_kernel(n_ref, a_ref, b_ref, zn_ref, zo_ref, src_ref, dst_ref, zero_scr, sem):
    t = pl.program_id(0)

    @pl.when(t == 0)
    def _():
        zero_scr[...] = jnp.zeros_like(zero_scr)

    for wait in (False, True):
        def per_expert(e, carry):
            i = t * N_EXPERTS + e
            _copy_run(src_ref, dst_ref, sem, n_ref[i], a_ref[i], b_ref[i], wait)
            return carry

        lax.fori_loop(0, N_EXPERTS, per_expert, 0)

        @pl.when(t == 0)
        def _():
            def per_gap(g, carry):
                def per_piece(i, c):
                    left = jnp.minimum(zn_ref[g] - i * TILE, TILE)
                    _copy_run(zero_scr, dst_ref, sem, left, 0, zo_ref[g] + i * TILE, wait)
                    return c

                lax.fori_loop(0, (zn_ref[g] + TILE - 1) // TILE, per_piece, 0)
                return carry

            lax.fori_loop(0, zn_ref.shape[0], per_gap, 0)


def _dispatch(run_n, run_a, run_b, fill, h_tiles, n_slots):
    n_tiles = run_n.shape[0] // N_EXPERTS
    grid_spec = pltpu.PrefetchScalarGridSpec(
        num_scalar_prefetch=5,
        grid=(n_tiles,),
        in_specs=[pl.BlockSpec((TOP_K * TILE, ROW_SUB, LANES), lambda t, *_: (t, 0, 0))],
        out_specs=pl.BlockSpec(memory_space=pl.ANY),
        scratch_shapes=[pltpu.VMEM((TILE, ROW_SUB, LANES), h_tiles.dtype), pltpu.SemaphoreType.DMA(())],
    )
    return pl.pallas_call(
        _dispatch_kernel,
        grid_spec=grid_spec,
        out_shape=jax.ShapeDtypeStruct((n_slots, ROW_SUB, LANES), h_tiles.dtype),
        compiler_params=_cparams(1, VMEM_LIMIT),
        name="moe_dispatch",
    )(run_n, run_a, run_b, fill[0], fill[1], h_tiles)


def _expert_kernel(be_ref, nv_ref, x_ref, wgu_ref, bgu_ref, wd_ref, bd_ref, y_ref, wgu_scr, wd_scr):
    i = pl.program_id(0)

    @pl.when(i < nv_ref[0])
    def _():
        prev = be_ref[jnp.maximum(i - 1, 0)]

        @pl.when((i == 0) | (prev != be_ref[i]))
        def _():
            wgu_scr[...] = wgu_ref[0, 0].astype(BF16)
            wd_scr[...] = wd_ref[0, 0].astype(BF16)

        x = x_ref[...].reshape(MOE_BLOCK, D_MODEL)
        gu = _dot(x, wgu_scr[...]) + bgu_ref[0, 0]
        gate = jnp.minimum(gu[:, 0:D_EXPERT], SWIGLU_LIMIT)
        up = jnp.clip(gu[:, D_EXPERT:], -SWIGLU_LIMIT, SWIGLU_LIMIT)
        act = (up + 1.0) * gate * (1.0 / (1.0 + jnp.exp(-SWIGLU_ALPHA * gate)))
        y = _dot(act.astype(BF16), wd_scr[...]) + bd_ref[0, 0]
        y_ref[...] = y.astype(BF16).reshape(MOE_BLOCK, ROW_SUB, LANES)

    @pl.when(i >= nv_ref[0])
    def _():
        y_ref[...] = jnp.zeros_like(y_ref)


def _experts(layer, block_e, n_valid, xs_sorted, w_gu, b_gu, w_down, b_down):
    n_slots = xs_sorted.shape[0]
    nb = n_slots // MOE_BLOCK
    depth = w_gu.shape[0]
    clamp = lambda i, nv: jnp.minimum(i, jnp.maximum(nv[0] - 1, 0))
    grid_spec = pltpu.PrefetchScalarGridSpec(
        num_scalar_prefetch=2,
        grid=(nb,),
        in_specs=[pl.BlockSpec((MOE_BLOCK, ROW_SUB, LANES), lambda i, be, nv: (clamp(i, nv), 0, 0)),
                  pl.BlockSpec((1, 1, D_MODEL, 2 * D_EXPERT), lambda i, be, nv: (layer, be[i], 0, 0)),
                  pl.BlockSpec((1, 1, 1, 2 * D_EXPERT), lambda i, be, nv: (layer, be[i], 0, 0)),
                  pl.BlockSpec((1, 1, D_EXPERT, D_MODEL), lambda i, be, nv: (layer, be[i], 0, 0)),
                  pl.BlockSpec((1, 1, 1, D_MODEL), lambda i, be, nv: (layer, be[i], 0, 0))],
        out_specs=pl.BlockSpec((MOE_BLOCK, ROW_SUB, LANES), lambda i, be, nv: (i, 0, 0)),
        scratch_shapes=[pltpu.VMEM((D_MODEL, 2 * D_EXPERT), BF16),
                        pltpu.VMEM((D_EXPERT, D_MODEL), BF16)],
    )
    return pl.pallas_call(
        _expert_kernel,
        grid_spec=grid_spec,
        out_shape=jax.ShapeDtypeStruct((n_slots, ROW_SUB, LANES), BF16),
        compiler_params=_cparams(1, VMEM_LIMIT),
        name="moe_experts",
    )(block_e, n_valid, xs_sorted, w_gu, b_gu.reshape(depth, N_EXPERTS, 1, 2 * D_EXPERT),
      w_down, b_down.reshape(depth, N_EXPERTS, 1, D_MODEL))


def _combine_kernel(n_ref, a_ref, b_ref, x_ref, mod_ref, gate_ref, pos_ref, y_ref, o_ref, buf, sem):
    t = pl.program_id(0) * pl.num_programs(1) + pl.program_id(1)
    n_tiles = pl.num_programs(0) * pl.num_programs(1)
    cur = t % 2

    def runs(tile, which, wait):
        def per_expert(e, carry):
            i = tile * N_EXPERTS + e
            _copy_run(y_ref, buf.at[which], sem.at[which], n_ref[i], b_ref[i], a_ref[i], wait)
            return carry

        lax.fori_loop(0, N_EXPERTS, per_expert, 0)

    @pl.when(t == 0)
    def _():
        runs(t, cur, False)

    @pl.when(t + 1 < n_tiles)
    def _():
        runs(t + 1, 1 - cur, False)

    runs(t, cur, True)

    gates = gate_ref[0]
    pos = pos_ref[0].astype(F32)
    slot_lane = lax.broadcasted_iota(I32, (TILE, TOP_K * TILE), 1).astype(F32)
    wmat = jnp.zeros((TILE, TOP_K * TILE), F32)
    for kk in range(TOP_K):
        wmat = wmat + jnp.where(slot_lane == pos[:, kk:kk + 1], gates[:, kk:kk + 1], 0.0)
    f = _dot(wmat.astype(BF16), buf[cur].reshape(TOP_K * TILE, D_MODEL))
    gate2 = mod_ref[0][:, 5 * D_MODEL:6 * D_MODEL]
    o_ref[0] = x_ref[0] + gate2 * f


def _combine(run_n, run_a, run_b, xs, mod_l, gates, pos, y_sorted):
    batch, rows, _ = xs.shape
    nt = rows // TILE
    tok = lambda w: pl.BlockSpec((1, TILE, w), lambda b, j, *_: (b, j, 0))
    mod_spec = _mod_spec(batch)
    grid_spec = pltpu.PrefetchScalarGridSpec(
        num_scalar_prefetch=3,
        grid=(batch, nt),
        in_specs=[tok(D_MODEL),
                  pl.BlockSpec(mod_spec.block_shape, lambda b, j, *_: mod_spec.index_map(b, j)),
                  tok(LANES), tok(2 * TOP_K), pl.BlockSpec(memory_space=pl.ANY)],
        out_specs=tok(D_MODEL),
        scratch_shapes=[pltpu.VMEM((2, TOP_K * TILE, ROW_SUB, LANES), y_sorted.dtype),
                        pltpu.SemaphoreType.DMA((2,))],
    )
    return pl.pallas_call(
        _combine_kernel,
        grid_spec=grid_spec,
        out_shape=jax.ShapeDtypeStruct((batch, rows, D_MODEL), F32),
        compiler_params=_cparams(2, VMEM_LIMIT),
        name="moe_combine",
    )(run_n, run_a, run_b, xs, mod_l, gates, pos, y_sorted)


def _moe_plan(counts, tables, n_tokens):
    cnt = counts[0, :N_EXPERTS].astype(I32)
    padded = (cnt + MOE_BLOCK - 1) // MOE_BLOCK * MOE_BLOCK
    pend = jnp.cumsum(padded)
    pstart = pend - padded
    nb = -(-(n_tokens * TOP_K) // MOE_BLOCK) + N_EXPERTS
    block_row = jnp.arange(nb, dtype=I32) * MOE_BLOCK
    block_e = jnp.minimum(jnp.sum((pend[None, :] <= block_row[:, None]).astype(I32), axis=1), N_EXPERTS - 1)
    n_valid = (pend[-1] // MOE_BLOCK).astype(I32).reshape(1)
    run_n = tables[:, 0, :N_EXPERTS]
    tile_pos = tables[:, 1, :N_EXPERTS]
    slot_pos = tables[:, 2, :N_EXPERTS] + pstart[None, :]
    fill_n = jnp.concatenate([padded - cnt, (nb * MOE_BLOCK - pend[-1])[None]])
    fill_at = jnp.concatenate([pstart + cnt, pend[-1:]])
    flat = lambda a: a.reshape(-1).astype(I32)
    return (flat(run_n), flat(tile_pos), flat(slot_pos), (flat(fill_n), flat(fill_at)),
            block_e.astype(I32), n_valid, nb * MOE_BLOCK)


def _arrange_w_in(w_in_l):
    hg = w_in_l[:, 0:HG_IN]
    ml0 = HG_IN
    ml = w_in_l[:, ml0:ml0 + ML_IN]
    gt = w_in_l[:, ml0 + ML_IN:ml0 + ML_IN + 4 * N_HEADS]
    at = w_in_l[:, ml0 + ML_IN + 4 * N_HEADS:]
    gt = jnp.repeat(_gate_order(gt.T).T, HEAD_DIM, axis=1)
    return jnp.concatenate([hg, ml, gt, at], 1).astype(BF16)


def _gate_order(g):
    g = g.reshape((4, N_HEADS) + g.shape[1:]) if g.shape[0] == 4 * N_HEADS else g
    return jnp.concatenate([g[0], g[2], g[1], g[3]], axis=0)


def _hgrn2_lower_bounds(lb_logits):
    p = jax.nn.softmax(lb_logits.astype(F32), axis=1)
    cum = jnp.cumsum(p, axis=1)
    return cum - cum[:, :1]


def kernel(x, c, ctx, c_ctx, w_mod, b_mod, norm1_g, norm2_g, w_in, w_out, hg_lb_logits, hg_norm_g,
           ml_gate_bias, ml_norm_g, q_norm_g, k_norm_g, w_router, b_router, w_gu, b_gu, w_down, b_down):
    batch, seq, d = x.shape
    ctx_len = ctx.shape[1]
    depth = w_mod.shape[0]
    assert d == D_MODEL and ctx_len == TILE and seq % TILE == 0 and batch + 1 <= MOD_ROWS
    assert batch % SCAN_PAIR == 0
    rows = ctx_len + seq
    n_tokens = batch * rows

    cvec = jnp.concatenate([c, c_ctx[None, :], jnp.zeros((MOD_ROWS - batch - 1, d), F32)], 0)
    mod = _modulation(cvec, w_mod, b_mod).reshape(depth, MOD_ROWS, 1, N_MOD)
    hg_lb = _hgrn2_lower_bounds(hg_lb_logits)
    cos, sin = _rope_tables(rows, ctx_len)
    xs = jnp.concatenate([ctx, x], axis=1)

    for layer in range(depth):
        mod_l = mod[layer]
        p_hg, p_ml, p_gt, qn, kn, vb = _inproj(xs, mod_l, norm1_g[layer][None, :], _arrange_w_in(w_in[layer]),
                                               cos, sin, q_norm_g[layer][None, :], k_norm_g[layer][None, :])
        o_hg = _hgrn2(p_hg, hg_lb[:, layer])
        gb = jnp.repeat(_gate_order(ml_gate_bias[layer]), HEAD_DIM).reshape(4, 1, MIX_W)
        o_ml = _mlstm(p_ml, p_gt, gb[0::2], gb[1::2])
        att = _attention(qn, kn, vb)
        w_router_p = jnp.pad(w_router[layer], ((0, 0), (0, LANES - N_EXPERTS)))
        wr_hi = w_router_p.astype(BF16)
        wr_lo = (w_router_p - wr_hi.astype(F32)).astype(BF16)
        w_router_p = jnp.concatenate([wr_hi, wr_hi, wr_lo], axis=0)
        b_router_p = jnp.pad(b_router[layer], (0, LANES - N_EXPERTS))[None, :]
        xs, h_tiles, pos, gates, tables, counts = _outproj(
            xs, mod_l, o_hg, p_hg, o_ml, p_ml, att,
            jnp.tile(hg_norm_g[layer], N_HEADS)[None, :], jnp.tile(ml_norm_g[layer], N_HEADS)[None, :],
            w_out[layer].astype(BF16), norm2_g[layer][None, :], w_router_p, b_router_p)
        run_n, tile_pos, slot_pos, fill, block_e, n_valid, n_slots = _moe_plan(counts, tables, n_tokens)
        xs_sorted = _dispatch(run_n, tile_pos, slot_pos, fill, h_tiles, n_slots)
        y_sorted = _experts(layer, block_e, n_valid, xs_sorted, w_gu, b_gu, w_down, b_down)
        xs = _combine(run_n, tile_pos, slot_pos, xs, mod_l, gates, pos, y_sorted)
    return xs[:, ctx_len:, :]
```

```python
import functools

import numpy as np
import jax
import jax.numpy as jnp
from jax import lax
from jax.experimental import pallas as pl
from jax.experimental.pallas import tpu as pltpu

F32 = jnp.float32
BF16 = jnp.bfloat16
I32 = jnp.int32
HIGHEST = lax.Precision.HIGHEST

D_MODEL = 1024
NORM_EPS = 1e-6
CHUNK = 64
TILE = 256
CHUNKS_PER_TILE = TILE // CHUNK
SCAN_PAIR = 4
GRID_W = 64
ROPE_THETA = 10000.0

N_HEADS = 4
HEAD_DIM = 64
MIX_W = N_HEADS * HEAD_DIM
AT_HEADS = 4
AT_KV_HEADS = 2
AT_HEAD_DIM = 128
AT_GROUP = AT_HEADS // AT_KV_HEADS
ATT_KEY_CHUNK = 2176

HG_IN = 5 * MIX_W
ML_IN = 4 * MIX_W
AT_IN = (AT_HEADS + 2 * AT_KV_HEADS) * AT_HEAD_DIM
GATE_W = 4 * MIX_W

N_EXPERTS = 32
TOP_K = 4
D_EXPERT = 1024
SWIGLU_LIMIT = 7.0
SWIGLU_ALPHA = 1.702
MOE_BLOCK = 512
ROW_SUB = D_MODEL // 128
LANES = 128
N_MOD = 6 * D_MODEL
MOD_ROWS = 16

VMEM_LIMIT = 56 * 1024 * 1024


def _cparams(n_axes, vmem=None):
    kw = dict(dimension_semantics=("arbitrary",) * n_axes)
    if vmem is not None:
        kw["vmem_limit_bytes"] = vmem
    return pltpu.CompilerParams(**kw)


def _dot(a, b, precision=None):
    return jnp.dot(a, b, preferred_element_type=F32, precision=precision)


def _dot_nt(a, b):
    return lax.dot_general(a, b, (((1,), (1,)), ((), ())), preferred_element_type=F32)


def _dot_tn(a, b, precision=None):
    return lax.dot_general(a, b, (((0,), (0,)), ((), ())), preferred_element_type=F32,
                           precision=precision)


def _log_sigmoid(z):
    return jnp.minimum(z, 0.0) - jnp.log1p(jnp.exp(-jnp.abs(z)))


def _rms(x, eps=NORM_EPS):
    return x * lax.rsqrt(jnp.mean(x * x, axis=-1, keepdims=True) + eps)


def _mod_kernel(c_ref, w_ref, b_ref, o_ref):
    cv = c_ref[...]
    s = cv * (1.0 / (1.0 + jnp.exp(-cv)))
    o_ref[0] = _dot(s, w_ref[0], precision=HIGHEST) + b_ref[0]


def _modulation(cvec, w_mod, b_mod):
    depth = w_mod.shape[0]
    nblk = 1536
    return pl.pallas_call(
        _mod_kernel,
        grid=(depth, N_MOD // nblk),
        in_specs=[pl.BlockSpec((MOD_ROWS, D_MODEL), lambda l, n: (0, 0)),
                  pl.BlockSpec((1, D_MODEL, nblk), lambda l, n: (l, 0, n)),
                  pl.BlockSpec((1, 1, nblk), lambda l, n: (l, 0, n))],
        out_specs=pl.BlockSpec((1, MOD_ROWS, nblk), lambda l, n: (l, 0, n)),
        out_shape=jax.ShapeDtypeStruct((depth, MOD_ROWS, N_MOD), F32),
        compiler_params=_cparams(2, VMEM_LIMIT),
        name="modulation",
    )(cvec, w_mod, b_mod.reshape(depth, 1, N_MOD))


def _mod_spec(batch):
    return pl.BlockSpec((1, 1, N_MOD), lambda b, j: (jnp.where(j == 0, batch, b), 0, 0))


def _inproj_kernel(x_ref, mod_ref, g_ref, w_ref, cos_ref, sin_ref, qg_ref, kg_ref,
                   hg_ref, ml_ref, gt_ref, q_ref, k_ref, v_ref):
    x = x_ref[0]
    mod = mod_ref[0]
    sh = mod[:, 0:D_MODEL]
    sc = mod[:, D_MODEL:2 * D_MODEL]
    h = _rms(x) * g_ref[...]
    h = (h * (1.0 + sc) + sh).astype(BF16)
    o0 = 0
    for ref, width in ((hg_ref, HG_IN), (ml_ref, ML_IN), (gt_ref, GATE_W)):
        ref[0] = _dot(h, w_ref[:, o0:o0 + width])
        o0 += width

    at = _dot(h, w_ref[:, o0:o0 + AT_IN])
    cos = cos_ref[...]
    sin = sin_ref[...]
    lane = lax.broadcasted_iota(I32, (TILE, AT_HEAD_DIM), 1)
    quarter = AT_HEAD_DIM // 4
    first_half = (lane % (2 * quarter)) < quarter

    def norm_rope(y, g):
        y = _rms(y) * g
        partner = jnp.where(first_half, pltpu.roll(y, AT_HEAD_DIM - quarter, axis=1), pltpu.roll(y, quarter, axis=1))
        return y * cos + partner * sin

    head = lambda i: at[:, i * AT_HEAD_DIM:(i + 1) * AT_HEAD_DIM]
    for i in range(AT_HEADS):
        q_ref[0, :, i * AT_HEAD_DIM:(i + 1) * AT_HEAD_DIM] = (
            norm_rope(head(i), qg_ref[...]) * (AT_HEAD_DIM ** -0.5)).astype(BF16)
    for i in range(AT_KV_HEADS):
        k_ref[0, :, i * AT_HEAD_DIM:(i + 1) * AT_HEAD_DIM] = norm_rope(head(AT_HEADS + i), kg_ref[...]).astype(BF16)
        v_ref[0, :, 2 * i * AT_HEAD_DIM:(2 * i + 1) * AT_HEAD_DIM] = head(AT_HEADS + AT_KV_HEADS + i).astype(BF16)
        v_ref[0, :, (2 * i + 1) * AT_HEAD_DIM:(2 * i + 2) * AT_HEAD_DIM] = jnp.ones((TILE, AT_HEAD_DIM), BF16)


def _inproj(xs, mod_l, g1, w_in_r, cos, sin, q_g, k_g):
    batch, rows, _ = xs.shape
    nt = rows // TILE
    n_all = HG_IN + ML_IN + GATE_W + AT_IN
    tok = lambda w: pl.BlockSpec((1, TILE, w), lambda b, j: (b, j, 0))
    tab = pl.BlockSpec((TILE, AT_HEAD_DIM), lambda b, j: (j, 0))
    vec = pl.BlockSpec((1, AT_HEAD_DIM), lambda b, j: (0, 0))
    kvw = AT_KV_HEADS * AT_HEAD_DIM
    widths = ((HG_IN, F32), (ML_IN, F32), (GATE_W, F32), (AT_HEADS * AT_HEAD_DIM, BF16), (kvw, BF16), (2 * kvw, BF16))
    return pl.pallas_call(
        _inproj_kernel,
        grid=(batch, nt),
        in_specs=[tok(D_MODEL), _mod_spec(batch),
                  pl.BlockSpec((1, D_MODEL), lambda b, j: (0, 0)),
                  pl.BlockSpec((D_MODEL, n_all), lambda b, j: (0, 0)),
                  tab, tab, vec, vec],
        out_specs=[tok(w) for w, _ in widths],
        out_shape=[jax.ShapeDtypeStruct((batch, rows, w), dt) for w, dt in widths],
        compiler_params=_cparams(2, VMEM_LIMIT),
        name="inproj",
    )(xs, mod_l, g1, w_in_r, cos, sin, q_g, k_g)


def _tile_of(d, j, nt):
    return j + d * jnp.where(j > 0, nt - 2 * j, 0)


def _chunk_of(d, i):
    return i + d * (CHUNKS_PER_TILE - 1 - 2 * i)


_HG_LEVELS = (32, 16, 8, 4, 2, 1)


def _head_block_mask():
    h = np.arange(MIX_W) // HEAD_DIM
    return (h[:, None] == h[None, :]).astype(np.float32)


def _split_hi_lo(x):
    hi = x.astype(BF16)
    lo = (x - hi.astype(F32)).astype(BF16)
    return jnp.concatenate([hi, lo], axis=0)


def _block_diag_rows(x, lane_head):
    zero = jnp.zeros_like(x)
    return jnp.concatenate([jnp.where(lane_head == h, x, zero) for h in range(N_HEADS)], axis=0)


@functools.lru_cache(maxsize=None)
def _hgrn2_consts():
    c = CHUNK
    tri = np.tril(np.ones((c, c)))
    ops = [tri]
    masks = []
    pos = np.arange(c)
    for m in _HG_LEVELS:
        blk = pos // (2 * m)
        sel = np.zeros((c, c))
        sel[pos, 2 * m * blk + m - 1] = 1.0
        ops.append(tri - sel @ tri)
        in_b = (pos % (2 * m)) >= m
        masks.append(((blk[:, None] == blk[None, :]) & in_b[:, None] & ~in_b[None, :]).astype(np.float64))
    masks.append(np.eye(c))
    ops.append(np.ones((c, c)) - tri)
    flip = np.eye(c)[::-1]
    stack = lambda xs: np.concatenate(xs, 0)
    ops_d = np.stack([stack(ops), stack([flip @ o @ flip for o in ops])])
    ops_d = np.concatenate([ops_d, ops_d], axis=2)
    tile = lambda mk: np.tile(mk, (1, N_HEADS))
    masks_d = np.stack([np.stack([tile(mk) for mk in masks]), np.stack([tile(mk[::-1, ::-1]) for mk in masks])])
    return ops_d.astype(np.float32), masks_d.astype(np.float32)


def _hgrn2_kernel(q_ref, z_ref, v_ref, lb_ref, ops_ref, msk_ref, hm_ref, o_ref, st_scr, lf_scr, k_scr, *,
                  direction):
    d = direction
    j = pl.program_id(1)

    @pl.when(j == 0)
    def _():
        st_scr[...] = jnp.zeros_like(st_scr)

    lb = lb_ref[0]
    z = z_ref[...]
    a = jnp.log(lb)
    cterm = jnp.log1p(-lb) + _log_sigmoid(z)
    lf_scr[...] = jnp.maximum(a, cterm) + jnp.log1p(jnp.exp(-jnp.abs(a - cterm)))
    k_scr[...] = (1.0 - lb) / (1.0 + jnp.exp(z))
    n_lv = len(_HG_LEVELS)
    lane_head = lax.broadcasted_iota(I32, (CHUNK, MIX_W), 1) // HEAD_DIM

    for i, bb in [(i, bb) for i in range(CHUNKS_PER_TILE) for bb in range(SCAN_PAIR)]:
        rows = pl.ds(_chunk_of(d, i) * CHUNK, CHUNK)
        cs = _dot(ops_ref[0], _split_hi_lo(lf_scr[bb, rows, :]))
        b = cs[0:CHUNK]
        rest = cs[(n_lv + 1) * CHUNK:(n_lv + 2) * CHUNK]
        q = q_ref[bb, rows, :]
        k = k_scr[bb, rows, :]
        vb = v_ref[bb, rows, :].astype(BF16)
        q_in = (q * jnp.exp(b)).astype(BF16)
        k_out = (k * jnp.exp(rest)).astype(BF16)
        s_decay = jnp.exp(b[0:1] + rest[0:1])
        sc = jnp.zeros((CHUNK, MIX_W), F32)
        for lv in range(n_lv + 1):
            if lv < n_lv:
                e = jnp.exp(-jnp.abs(cs[(lv + 1) * CHUNK:(lv + 2) * CHUNK]))
                ql = (q * e).astype(BF16)
                kl = (k * e).astype(BF16)
            else:
                ql = q.astype(BF16)
                kl = k.astype(BF16)
            sc = sc + msk_ref[0, lv] * _dot_nt(ql, _block_diag_rows(kl, lane_head))
        st = st_scr[bb]
        o_ref[bb, rows, :] = (_dot(sc.astype(BF16), _block_diag_rows(vb, lane_head))
                              + _dot_nt(q_in, st.astype(BF16)))
        st_scr[bb] = st * s_decay + hm_ref[...] * _dot_tn(vb, k_out)


def _hgrn2(p_hg, lb_l):
    batch, rows, _ = p_hg.shape
    nt = rows // TILE
    ops_d, masks_d = _hgrn2_consts()
    n_ops = ops_d.shape[1]
    n_msk = masks_d.shape[1]
    def one_direction(d):
        col = lambda c: pl.BlockSpec((SCAN_PAIR, TILE, MIX_W), lambda b, j: (b, _tile_of(d, j, nt), c))
        const = lambda shp: pl.BlockSpec((1,) + shp, lambda b, j: (d,) + (0,) * len(shp))
        return pl.pallas_call(
            functools.partial(_hgrn2_kernel, direction=d),
            grid=(batch // SCAN_PAIR, nt),
            in_specs=[col(0), col(1 + d), col(3),
                      const((1, MIX_W)), const((n_ops, 2 * CHUNK)), const((n_msk, CHUNK, MIX_W)),
                      pl.BlockSpec((MIX_W, MIX_W), lambda b, j: (0, 0))],
            out_specs=col(0),
            out_shape=jax.ShapeDtypeStruct((batch, rows, MIX_W), F32),
            scratch_shapes=[pltpu.VMEM((SCAN_PAIR, MIX_W, MIX_W), F32),
                            pltpu.VMEM((SCAN_PAIR, TILE, MIX_W), F32),
                            pltpu.VMEM((SCAN_PAIR, TILE, MIX_W), F32)],
            compiler_params=_cparams(2, VMEM_LIMIT),
            name="hgrn2",
        )(p_hg, p_hg, p_hg, lb_l.reshape(2, 1, MIX_W), jnp.asarray(ops_d, dtype=BF16), jnp.asarray(masks_d),
          jnp.asarray(_head_block_mask()))

    return one_direction(0), one_direction(1)


@functools.lru_cache(maxsize=None)
def _mlstm_consts():
    c = CHUNK
    tri = np.tril(np.ones((c, c)))
    flip = np.eye(c)[::-1]
    tile = lambda mk: np.tile(mk, (1, N_HEADS))
    ops, masks = [], []
    for t in (tri, flip @ tri @ flip):
        ops.append(np.concatenate([t, t], axis=1))
        masks.append(np.stack([tile(t.T), tile(t), tile(np.eye(c))]))
    return np.stack(ops).astype(np.float32), np.stack(masks).astype(np.float32)


def _mlstm_kernel(q_ref, k_ref, v_ref, gi_ref, gf_ref, bi_ref, bf_ref, op_ref, cm_ref, hm_ref, h_ref,
                  ct_scr, n_scr, m_scr, i_scr, f_scr, *, direction):
    d = direction
    j = pl.program_id(1)

    @pl.when(j == 0)
    def _():
        ct_scr[...] = jnp.zeros_like(ct_scr)
        n_scr[...] = jnp.zeros_like(n_scr)
        m_scr[...] = jnp.zeros_like(m_scr)

    i_scr[...] = gi_ref[...] + bi_ref[0]
    f_scr[...] = _log_sigmoid(gf_ref[...] + bf_ref[0])
    lane_head = lax.broadcasted_iota(I32, (CHUNK, MIX_W), 1) // HEAD_DIM
    neg_inf = F32(-jnp.inf)
    hm = hm_ref[...]
    hm_b = hm.astype(BF16)

    for i, bb in [(i, bb) for i in range(CHUNKS_PER_TILE) for bb in range(SCAN_PAIR)]:
        rows = pl.ds(_chunk_of(d, i) * CHUNK, CHUNK)
        ie = i_scr[bb, rows, :]
        fe = f_scr[bb, rows, :]
        bc = _dot(op_ref[0], _split_hi_lo(fe))
        row = jnp.sum(ie * cm_ref[0, 2] - fe * cm_ref[0, 0], axis=0, keepdims=True)
        tot = jnp.sum(fe, axis=0, keepdims=True)
        m_st = m_scr[bb]
        dm = jnp.where(cm_ref[0, 1] > 0.5, bc + row, neg_inf)
        m_in = jnp.zeros((CHUNK, MIX_W), F32)
        for h in range(N_HEADS):
            mh = jnp.max(dm[:, h * HEAD_DIM:(h + 1) * HEAD_DIM], axis=-1, keepdims=True)
            m_in = jnp.where(lane_head == h, mh, m_in)
        from_prev = bc + m_st
        m_t = jnp.maximum(from_prev, m_in)
        q = q_ref[bb, rows, :]
        k = k_ref[bb, rows, :] * (HEAD_DIM ** -0.5)
        qb = q.astype(BF16)
        vb = v_ref[bb, rows, :].astype(BF16)
        w = jnp.exp(dm - m_t) * _dot_nt(qb, _block_diag_rows(k.astype(BF16), lane_head))
        a_prev = jnp.exp(from_prev - m_t)
        ct = ct_scr[bb]
        n_st = n_scr[bb]
        num = _dot(w.astype(BF16), _block_diag_rows(vb, lane_head)) + a_prev * _dot_nt(qb, ct.astype(BF16))
        den = _dot((w + a_prev * (q * n_st)).astype(BF16), hm_b)
        h_ref[bb, rows, :] = num / jnp.maximum(jnp.abs(den), jnp.exp(-m_t))
        e_col = tot - bc + ie
        m_new = jnp.maximum(tot + m_st, jnp.max(e_col, axis=0, keepdims=True))
        a_end = jnp.exp(tot + m_st - m_new)
        kw = k * jnp.exp(e_col - m_new)
        ct_scr[bb] = ct * a_end + hm * _dot_tn(vb, kw.astype(BF16))
        n_scr[bb] = a_end * n_st + jnp.sum(kw, axis=0, keepdims=True)
        m_scr[bb] = m_new


def _mlstm(p_ml, p_gt, bias_i, bias_f):
    batch, rows, _ = p_ml.shape
    nt = rows // TILE
    ops_d, masks_d = _mlstm_consts()
    def one_direction(d):
        col = lambda c: pl.BlockSpec((SCAN_PAIR, TILE, MIX_W), lambda b, j: (b, _tile_of(d, j, nt), c))
        const = lambda shp: pl.BlockSpec((1,) + shp, lambda b, j: (d,) + (0,) * len(shp))
        return pl.pallas_call(
            functools.partial(_mlstm_kernel, direction=d),
            grid=(batch // SCAN_PAIR, nt),
            in_specs=[col(0), col(1), col(2), col(2 * d), col(2 * d + 1),
                      const((1, MIX_W)), const((1, MIX_W)),
                      const((CHUNK, 2 * CHUNK)), const((3, CHUNK, MIX_W)),
                      pl.BlockSpec((MIX_W, MIX_W), lambda b, j: (0, 0))],
            out_specs=col(0),
            out_shape=jax.ShapeDtypeStruct((batch, rows, MIX_W), F32),
            scratch_shapes=[pltpu.VMEM((SCAN_PAIR, MIX_W, MIX_W), F32),
                            pltpu.VMEM((SCAN_PAIR, 1, MIX_W), F32),
                            pltpu.VMEM((SCAN_PAIR, 1, MIX_W), F32),
                            pltpu.VMEM((SCAN_PAIR, TILE, MIX_W), F32),
                            pltpu.VMEM((SCAN_PAIR, TILE, MIX_W), F32)],
            compiler_params=_cparams(2, VMEM_LIMIT),
            name="mlstm",
        )(p_ml, p_ml, p_ml, p_gt, p_gt, bias_i, bias_f, jnp.asarray(ops_d, dtype=BF16), jnp.asarray(masks_d),
          jnp.asarray(_head_block_mask()))

    return one_direction(0), one_direction(1)


def _rope_tables(rows, ctx_len):
    n = rows - ctx_len
    r = jnp.arange(n) // GRID_W
    cc = jnp.arange(n) % GRID_W
    axis_dims = AT_HEAD_DIM // 2
    inv_freq = ROPE_THETA ** (-jnp.arange(0, axis_dims, 2, dtype=F32) / axis_dims)
    ang_r = r.astype(F32)[:, None] * inv_freq
    ang_c = cc.astype(F32)[:, None] * inv_freq
    cos = jnp.concatenate([jnp.cos(ang_r), jnp.cos(ang_r), jnp.cos(ang_c), jnp.cos(ang_c)], -1)
    sin = jnp.concatenate([-jnp.sin(ang_r), jnp.sin(ang_r), -jnp.sin(ang_c), jnp.sin(ang_c)], -1)
    cos = jnp.concatenate([jnp.ones((ctx_len, AT_HEAD_DIM), F32), cos], 0)
    sin = jnp.concatenate([jnp.zeros((ctx_len, AT_HEAD_DIM), F32), sin], 0)
    return cos, sin


def _attn_kernel(q_ref, k_ref, v_ref, o_ref):
    j = pl.program_id(2)

    def attend(n_keys):
        q2 = jnp.concatenate([q_ref[0, :, g * AT_HEAD_DIM:(g + 1) * AT_HEAD_DIM] for g in range(AT_GROUP)], axis=0)
        m = jnp.full((AT_GROUP * TILE, 1), -jnp.inf, F32)
        acc = jnp.zeros((AT_GROUP * TILE, 2 * AT_HEAD_DIM), F32)
        bounds = list(range(0, n_keys, ATT_KEY_CHUNK))
        bounds = bounds[:max(len(bounds) - 1, 1)] if n_keys % ATT_KEY_CHUNK else bounds
        bounds.append(n_keys)
        for c0, c1 in zip(bounds[:-1], bounds[1:]):
            s = _dot_nt(q2, k_ref[0, c0:c1, :])
            m_new = jnp.maximum(m, jnp.max(s, axis=-1, keepdims=True))
            p = jnp.exp((s - m_new).astype(BF16))
            acc = acc * jnp.exp(m - m_new) + _dot(p, v_ref[0, c0:c1, :])
            m = m_new
        o = acc[:, 0:AT_HEAD_DIM] / acc[:, AT_HEAD_DIM:AT_HEAD_DIM + 1]
        for g in range(AT_GROUP):
            o_ref[0, :, g * AT_HEAD_DIM:(g + 1) * AT_HEAD_DIM] = o[g * TILE:(g + 1) * TILE]

    @pl.when(j == 0)
    def _():
        attend(TILE)

    @pl.when(j > 0)
    def _():
        attend(k_ref.shape[1])


def _attention(qn, kn, vb):
    batch, rows, _ = qn.shape
    nt = rows // TILE
    gw = AT_GROUP * AT_HEAD_DIM
    return pl.pallas_call(
        _attn_kernel,
        grid=(batch, AT_KV_HEADS, nt),
        in_specs=[pl.BlockSpec((1, TILE, gw), lambda b, h, j: (b, j, h)),
                  pl.BlockSpec((1, rows, AT_HEAD_DIM), lambda b, h, j: (b, 0, h)),
                  pl.BlockSpec((1, rows, 2 * AT_HEAD_DIM), lambda b, h, j: (b, 0, h))],
        out_specs=pl.BlockSpec((1, TILE, gw), lambda b, h, j: (b, j, h)),
        out_shape=jax.ShapeDtypeStruct((batch, rows, AT_HEADS * AT_HEAD_DIM), F32),
        compiler_params=_cparams(3, VMEM_LIMIT),
        name="attention",
    )(qn, kn, vb)


def _outproj_kernel(x_ref, mod_ref, ohf_ref, ohb_ref, gh_ref, omf_ref, omb_ref, gm_ref, at_ref,
                    hgg_ref, mlg_ref, ones_ref, wo_ref, g2_ref, wr_ref, br_ref, low_ref, up_ref,
                    xo_ref, st_ref, pos_ref, gate_ref, tab_ref, cnt_ref, cnt_scr):
    b = pl.program_id(0)
    j = pl.program_id(1)

    @pl.when((b == 0) & (j == 0))
    def _():
        cnt_scr[...] = jnp.zeros_like(cnt_scr)

    mod = mod_ref[0]
    gate1 = mod[:, 2 * D_MODEL:3 * D_MODEL]
    sh2 = mod[:, 3 * D_MODEL:4 * D_MODEL]
    sc2 = mod[:, 4 * D_MODEL:5 * D_MODEL]

    def head_norm(o, g):
        ms = _dot((o * o).astype(BF16), ones_ref[...]) * (1.0 / HEAD_DIM)
        return o * lax.rsqrt(ms + NORM_EPS) * g

    gh = gh_ref[0]
    y_hg = head_norm(ohf_ref[0] + ohb_ref[0], hgg_ref[...]) * (gh / (1.0 + jnp.exp(-gh)))
    y_ml = head_norm(omf_ref[0] + omb_ref[0], mlg_ref[...]) / (1.0 + jnp.exp(-gm_ref[0]))
    proj = (_dot(y_hg.astype(BF16), wo_ref[0:MIX_W, :])
            + _dot(y_ml.astype(BF16), wo_ref[MIX_W:2 * MIX_W, :])
            + _dot(at_ref[0].astype(BF16), wo_ref[2 * MIX_W:, :]))
    x = x_ref[0] + gate1 * proj
    xo_ref[0] = x
    h2 = _rms(x) * g2_ref[...]
    h2 = h2 * (1.0 + sc2) + sh2

    lane = lax.broadcasted_iota(I32, (TILE, LANES), 1)
    lane_f = lane.astype(F32)
    h_hi = h2.astype(BF16)
    h_lo = (h2 - h_hi.astype(F32)).astype(BF16)
    logits = _dot(jnp.concatenate([h_hi, h_lo, h_hi], axis=1), wr_ref[...]) + br_ref[...]
    lg = jnp.where(lane < N_EXPERTS, logits, -jnp.inf)
    multi = jnp.zeros((TILE, LANES), F32)
    hots, vals = [], []
    for _ in range(TOP_K):
        mx = jnp.max(lg, axis=-1, keepdims=True)
        idx = jnp.min(jnp.where(lg == mx, lane_f, F32(LANES)), axis=-1, keepdims=True)
        hot = lane_f == idx
        lg = jnp.where(hot, -jnp.inf, lg)
        multi = multi + hot.astype(F32)
        hots.append(hot)
        vals.append(mx)
    es = [jnp.exp(v - vals[0]) for v in vals]
    den = es[0] + es[1] + es[2] + es[3]

    tile_cnt = jnp.sum(multi, axis=0, keepdims=True)
    tile_start = _dot(jnp.broadcast_to(tile_cnt, (8, LANES)).astype(BF16), up_ref[...])[0:1]
    where_in_tile = _dot(low_ref[...], multi.astype(BF16)) + tile_start
    slot_lane = lax.broadcasted_iota(I32, (TILE, TOP_K * TILE), 1).astype(F32)
    onehot = jnp.zeros((TILE, TOP_K * TILE), F32)
    pos_out = jnp.zeros((TILE, LANES), I32)
    gate_out = jnp.zeros((TILE, LANES), F32)
    for kk in range(TOP_K):
        pos = jnp.sum(jnp.where(hots[kk], where_in_tile, 0.0), axis=-1, keepdims=True)
        onehot = onehot + jnp.where(slot_lane == pos, 1.0, 0.0)
        pos_out = pos_out + jnp.where(lane == kk, pos.astype(I32), 0)
        gate_out = gate_out + jnp.where(lane == kk, es[kk] / den, 0.0)
    sorted_rows = _dot_tn(onehot.astype(BF16), h_hi)
    st_ref[...] = sorted_rows.astype(BF16).reshape(TOP_K * TILE, ROW_SUB, LANES)
    pos_ref[0] = pos_out[:, 0:2 * TOP_K]
    gate_ref[0] = gate_out
    tab_ref[0] = jnp.concatenate([tile_cnt, tile_start, cnt_scr[...], jnp.zeros((5, LANES), F32)], axis=0).astype(I32)
    cnt_scr[...] = cnt_scr[...] + tile_cnt
    cnt_ref[...] = cnt_scr[...]


def _outproj(xs, mod_l, o_hg, p_hg, o_ml, p_ml, att, hg_g, ml_g, w_out_b, g2, w_router_p, b_router_p):
    batch, rows, _ = xs.shape
    nt = rows // TILE
    tok = lambda w, c=0: pl.BlockSpec((1, TILE, w), lambda b, j: (b, j, c))
    dirs = lambda d: tok(MIX_W)
    full = lambda r, c: pl.BlockSpec((r, c), lambda b, j: (0, 0))
    block_ones = np.kron(np.eye(N_HEADS), np.ones((HEAD_DIM, HEAD_DIM))).astype(np.float32)
    strict_low = np.tril(np.ones((TILE, TILE)), -1).astype(np.float32)
    strict_up = np.triu(np.ones((LANES, LANES)), 1).astype(np.float32)
    n_tiles = batch * nt
    return pl.pallas_call(
        _outproj_kernel,
        grid=(batch, nt),
        in_specs=[tok(D_MODEL), _mod_spec(batch),
                  dirs(0), dirs(1), tok(MIX_W, 4),
                  dirs(0), dirs(1), tok(MIX_W, 3),
                  tok(AT_HEADS * AT_HEAD_DIM),
                  full(1, MIX_W), full(1, MIX_W), full(MIX_W, MIX_W),
                  full(D_MODEL, D_MODEL), full(1, D_MODEL),
                  full(3 * D_MODEL, LANES), full(1, LANES), full(TILE, TILE), full(LANES, LANES)],
        out_specs=[tok(D_MODEL),
                   pl.BlockSpec((TOP_K * TILE, ROW_SUB, LANES), lambda b, j: (b * nt + j, 0, 0)),
                   tok(2 * TOP_K), tok(LANES),
                   pl.BlockSpec((1, 8, LANES), lambda b, j: (b * nt + j, 0, 0)),
                   pl.BlockSpec((1, LANES), lambda b, j: (0, 0))],
        out_shape=[jax.ShapeDtypeStruct((batch, rows, D_MODEL), F32),
                   jax.ShapeDtypeStruct((n_tiles * TOP_K * TILE, ROW_SUB, LANES), BF16),
                   jax.ShapeDtypeStruct((batch, rows, 2 * TOP_K), I32),
                   jax.ShapeDtypeStruct((batch, rows, LANES), F32),
                   jax.ShapeDtypeStruct((n_tiles, 8, LANES), I32),
                   jax.ShapeDtypeStruct((1, LANES), F32)],
        scratch_shapes=[pltpu.VMEM((1, LANES), F32)],
        compiler_params=_cparams(2, VMEM_LIMIT),
        name="outproj_router",
    )(xs, mod_l, o_hg[0], o_hg[1], p_hg, o_ml[0], o_ml[1], p_ml, att, hg_g, ml_g,
      jnp.asarray(block_ones, dtype=BF16),
      w_out_b, g2, w_router_p, b_router_p, jnp.asarray(strict_low, dtype=BF16),
      jnp.asarray(strict_up, dtype=BF16))


_RUN_BITS = tuple(1 << s for s in range(TILE.bit_length() - 1, -1, -1))


def _copy_run(src, dst, sem, n, src0, dst0, wait):
    for bit in _RUN_BITS:
        done = n - (n & (2 * bit - 1))

        @pl.when((n & bit) != 0)
        def _():
            cp = pltpu.make_async_copy(src.at[pl.ds(src0 + done, bit)], dst.at[pl.ds(dst0 + done, bit)], sem)
            cp.wait() if wait else cp.start()


def _dispatch_kernel(n_ref, a_ref, b_ref, zn_ref, zo_ref, src_ref, dst_ref, zero_scr, sem):
    t = pl.program_id(0)

    @pl.when(t == 0)
    def _():
        zero_scr[...] = jnp.zeros_like(zero_scr)

    for wait in (False, True):
        def per_expert(e, carry):
            i = t * N_EXPERTS + e
            _copy_run(src_ref, dst_ref, sem, n_ref[i], a_ref[i], b_ref[i], wait)
            return carry

        lax.fori_loop(0, N_EXPERTS, per_expert, 0)

        @pl.when(t == 0)
        def _():
            def per_gap(g, carry):
                def per_piece(i, c):
                    left = jnp.minimum(zn_ref[g] - i * TILE, TILE)
                    _copy_run(zero_scr, dst_ref, sem, left, 0, zo_ref[g] + i * TILE, wait)
                    return c

                lax.fori_loop(0, (zn_ref[g] + TILE - 1) // TILE, per_piece, 0)
                return carry

            lax.fori_loop(0, zn_ref.shape[0], per_gap, 0)


def _dispatch(run_n, run_a, run_b, fill, h_tiles, n_slots):
    n_tiles = run_n.shape[0] // N_EXPERTS
    grid_spec = pltpu.PrefetchScalarGridSpec(
        num_scalar_prefetch=5,
        grid=(n_tiles,),
        in_specs=[pl.BlockSpec((TOP_K * TILE, ROW_SUB, LANES), lambda t, *_: (t, 0, 0))],
        out_specs=pl.BlockSpec(memory_space=pl.ANY),
        scratch_shapes=[pltpu.VMEM((TILE, ROW_SUB, LANES), h_tiles.dtype), pltpu.SemaphoreType.DMA(())],
    )
    return pl.pallas_call(
        _dispatch_kernel,
        grid_spec=grid_spec,
        out_shape=jax.ShapeDtypeStruct((n_slots, ROW_SUB, LANES), h_tiles.dtype),
        compiler_params=_cparams(1, VMEM_LIMIT),
        name="moe_dispatch",
    )(run_n, run_a, run_b, fill[0], fill[1], h_tiles)


def _expert_kernel(be_ref, nv_ref, x_ref, wgu_ref, bgu_ref, wd_ref, bd_ref, y_ref, wgu_scr, wd_scr):
    i = pl.program_id(0)

    @pl.when(i < nv_ref[0])
    def _():
        prev = be_ref[jnp.maximum(i - 1, 0)]

        @pl.when((i == 0) | (prev != be_ref[i]))
        def _():
            wgu_scr[...] = wgu_ref[0, 0].astype(BF16)
            wd_scr[...] = wd_ref[0, 0].astype(BF16)

        x = x_ref[...].reshape(MOE_BLOCK, D_MODEL)
        gu = _dot(x, wgu_scr[...]) + bgu_ref[0, 0]
        gate = jnp.minimum(gu[:, 0:D_EXPERT], SWIGLU_LIMIT)
        up = jnp.clip(gu[:, D_EXPERT:], -SWIGLU_LIMIT, SWIGLU_LIMIT)
        act = (up + 1.0) * gate * (1.0 / (1.0 + jnp.exp(-SWIGLU_ALPHA * gate)))
        y = _dot(act.astype(BF16), wd_scr[...]) + bd_ref[0, 0]
        y_ref[...] = y.astype(BF16).reshape(MOE_BLOCK, ROW_SUB, LANES)

    @pl.when(i >= nv_ref[0])
    def _():
        y_ref[...] = jnp.zeros_like(y_ref)


def _experts(layer, block_e, n_valid, xs_sorted, w_gu, b_gu, w_down, b_down):
    n_slots = xs_sorted.shape[0]
    nb = n_slots // MOE_BLOCK
    depth = w_gu.shape[0]
    clamp = lambda i, nv: jnp.minimum(i, jnp.maximum(nv[0] - 1, 0))
    grid_spec = pltpu.PrefetchScalarGridSpec(
        num_scalar_prefetch=2,
        grid=(nb,),
        in_specs=[pl.BlockSpec((MOE_BLOCK, ROW_SUB, LANES), lambda i, be, nv: (clamp(i, nv), 0, 0)),
                  pl.BlockSpec((1, 1, D_MODEL, 2 * D_EXPERT), lambda i, be, nv: (layer, be[i], 0, 0)),
                  pl.BlockSpec((1, 1, 1, 2 * D_EXPERT), lambda i, be, nv: (layer, be[i], 0, 0)),
                  pl.BlockSpec((1, 1, D_EXPERT, D_MODEL), lambda i, be, nv: (layer, be[i], 0, 0)),
                  pl.BlockSpec((1, 1, 1, D_MODEL), lambda i, be, nv: (layer, be[i], 0, 0))],
        out_specs=pl.BlockSpec((MOE_BLOCK, ROW_SUB, LANES), lambda i, be, nv: (i, 0, 0)),
        scratch_shapes=[pltpu.VMEM((D_MODEL, 2 * D_EXPERT), BF16),
                        pltpu.VMEM((D_EXPERT, D_MODEL), BF16)],
    )
    return pl.pallas_call(
        _expert_kernel,
        grid_spec=grid_spec,
        out_shape=jax.ShapeDtypeStruct((n_slots, ROW_SUB, LANES), BF16),
        compiler_params=_cparams(1, VMEM_LIMIT),
        name="moe_experts",
    )(block_e, n_valid, xs_sorted, w_gu, b_gu.reshape(depth, N_EXPERTS, 1, 2 * D_EXPERT),
      w_down, b_down.reshape(depth, N_EXPERTS, 1, D_MODEL))


def _combine_kernel(n_ref, a_ref, b_ref, x_ref, mod_ref, gate_ref, pos_ref, y_ref, o_ref, buf, sem):
    t = pl.program_id(0) * pl.num_programs(1) + pl.program_id(1)
    n_tiles = pl.num_programs(0) * pl.num_programs(1)
    cur = t % 2

    def runs(tile, which, wait):
        def per_expert(e, carry):
            i = tile * N_EXPERTS + e
            _copy_run(y_ref, buf.at[which], sem.at[which], n_ref[i], b_ref[i], a_ref[i], wait)
            return carry

        lax.fori_loop(0, N_EXPERTS, per_expert, 0)

    @pl.when(t == 0)
    def _():
        runs(t, cur, False)

    @pl.when(t + 1 < n_tiles)
    def _():
        runs(t + 1, 1 - cur, False)

    runs(t, cur, True)

    gates = gate_ref[0]
    pos = pos_ref[0].astype(F32)
    slot_lane = lax.broadcasted_iota(I32, (TILE, TOP_K * TILE), 1).astype(F32)
    wmat = jnp.zeros((TILE, TOP_K * TILE), F32)
    for kk in range(TOP_K):
        wmat = wmat + jnp.where(slot_lane == pos[:, kk:kk + 1], gates[:, kk:kk + 1], 0.0)
    f = _dot(wmat.astype(BF16), buf[cur].reshape(TOP_K * TILE, D_MODEL))
    gate2 = mod_ref[0][:, 5 * D_MODEL:6 * D_MODEL]
    o_ref[0] = x_ref[0] + gate2 * f


def _combine(run_n, run_a, run_b, xs, mod_l, gates, pos, y_sorted):
    batch, rows, _ = xs.shape
    nt = rows // TILE
    tok = lambda w: pl.BlockSpec((1, TILE, w), lambda b, j, *_: (b, j, 0))
    mod_spec = _mod_spec(batch)
    grid_spec = pltpu.PrefetchScalarGridSpec(
        num_scalar_prefetch=3,
        grid=(batch, nt),
        in_specs=[tok(D_MODEL),
                  pl.BlockSpec(mod_spec.block_shape, lambda b, j, *_: mod_spec.index_map(b, j)),
                  tok(LANES), tok(2 * TOP_K), pl.BlockSpec(memory_space=pl.ANY)],
        out_specs=tok(D_MODEL),
        scratch_shapes=[pltpu.VMEM((2, TOP_K * TILE, ROW_SUB, LANES), y_sorted.dtype),
                        pltpu.SemaphoreType.DMA((2,))],
    )
    return pl.pallas_call(
        _combine_kernel,
        grid_spec=grid_spec,
        out_shape=jax.ShapeDtypeStruct((batch, rows, D_MODEL), F32),
        compiler_params=_cparams(2, VMEM_LIMIT),
        name="moe_combine",
    )(run_n, run_a, run_b, xs, mod_l, gates, pos, y_sorted)


def _moe_plan(counts, tables, n_tokens):
    cnt = counts[0, :N_EXPERTS].astype(I32)
    padded = (cnt + MOE_BLOCK - 1) // MOE_BLOCK * MOE_BLOCK
    pend = jnp.cumsum(padded)
    pstart = pend - padded
    nb = -(-(n_tokens * TOP_K) // MOE_BLOCK) + N_EXPERTS
    block_row = jnp.arange(nb, dtype=I32) * MOE_BLOCK
    block_e = jnp.minimum(jnp.sum((pend[None, :] <= block_row[:, None]).astype(I32), axis=1), N_EXPERTS - 1)
    n_valid = (pend[-1] // MOE_BLOCK).astype(I32).reshape(1)
    run_n = tables[:, 0, :N_EXPERTS]
    tile_pos = tables[:, 1, :N_EXPERTS]
    slot_pos = tables[:, 2, :N_EXPERTS] + pstart[None, :]
    fill_n = jnp.concatenate([padded - cnt, (nb * MOE_BLOCK - pend[-1])[None]])
    fill_at = jnp.concatenate([pstart + cnt, pend[-1:]])
    flat = lambda a: a.reshape(-1).astype(I32)
    return (flat(run_n), flat(tile_pos), flat(slot_pos), (flat(fill_n), flat(fill_at)),
            block_e.astype(I32), n_valid, nb * MOE_BLOCK)


def _arrange_w_in(w_in_l):
    hg = w_in_l[:, 0:HG_IN]
    ml0 = HG_IN
    ml = w_in_l[:, ml0:ml0 + ML_IN]
    gt = w_in_l[:, ml0 + ML_IN:ml0 + ML_IN + 4 * N_HEADS]
    at = w_in_l[:, ml0 + ML_IN + 4 * N_HEADS:]
    gt = jnp.repeat(_gate_order(gt.T).T, HEAD_DIM, axis=1)
    return jnp.concatenate([hg, ml, gt, at], 1).astype(BF16)


def _gate_order(g):
    g = g.reshape((4, N_HEADS) + g.shape[1:]) if g.shape[0] == 4 * N_HEADS else g
    return jnp.concatenate([g[0], g[2], g[1], g[3]], axis=0)


def _hgrn2_lower_bounds(lb_logits):
    p = jax.nn.softmax(lb_logits.astype(F32), axis=1)
    cum = jnp.cumsum(p, axis=1)
    return cum - cum[:, :1]


def kernel(x, c, ctx, c_ctx, w_mod, b_mod, norm1_g, norm2_g, w_in, w_out, hg_lb_logits, hg_norm_g,
           ml_gate_bias, ml_norm_g, q_norm_g, k_norm_g, w_router, b_router, w_gu, b_gu, w_down, b_down):
    batch, seq, d = x.shape
    ctx_len = ctx.shape[1]
    depth = w_mod.shape[0]
    assert d == D_MODEL and ctx_len == TILE and seq % TILE == 0 and batch + 1 <= MOD_ROWS
    assert batch % SCAN_PAIR == 0
    rows = ctx_len + seq
    n_tokens = batch * rows

    cvec = jnp.concatenate([c, c_ctx[None, :], jnp.zeros((MOD_ROWS - batch - 1, d), F32)], 0)
    mod = _modulation(cvec, w_mod, b_mod).reshape(depth, MOD_ROWS, 1, N_MOD)
    hg_lb = _hgrn2_lower_bounds(hg_lb_logits)
    cos, sin = _rope_tables(rows, ctx_len)
    xs = jnp.concatenate([ctx, x], axis=1)

    for layer in range(depth):
        mod_l = mod[layer]
        p_hg, p_ml, p_gt, qn, kn, vb = _inproj(xs, mod_l, norm1_g[layer][None, :], _arrange_w_in(w_in[layer]),
                                               cos, sin, q_norm_g[layer][None, :], k_norm_g[layer][None, :])
        o_hg = _hgrn2(p_hg, hg_lb[:, layer])
        gb = jnp.repeat(_gate_order(ml_gate_bias[layer]), HEAD_DIM).reshape(4, 1, MIX_W)
        o_ml = _mlstm(p_ml, p_gt, gb[0::2], gb[1::2])
        att = _attention(qn, kn, vb)
        w_router_p = jnp.pad(w_router[layer], ((0, 0), (0, LANES - N_EXPERTS)))
        wr_hi = w_router_p.astype(BF16)
        wr_lo = (w_router_p - wr_hi.astype(F32)).astype(BF16)
        w_router_p = jnp.concatenate([wr_hi, wr_hi, wr_lo], axis=0)
        b_router_p = jnp.pad(b_router[layer], (0, LANES - N_EXPERTS))[None, :]
        xs, h_tiles, pos, gates, tables, counts = _outproj(
            xs, mod_l, o_hg, p_hg, o_ml, p_ml, att,
            jnp.tile(hg_norm_g[layer], N_HEADS)[None, :], jnp.tile(ml_norm_g[layer], N_HEADS)[None, :],
            w_out[layer].astype(BF16), norm2_g[layer][None, :], w_router_p, b_router_p)
        run_n, tile_pos, slot_pos, fill, block_e, n_valid, n_slots = _moe_plan(counts, tables, n_tokens)
        xs_sorted = _dispatch(run_n, tile_pos, slot_pos, fill, h_tiles, n_slots)
        y_sorted = _experts(layer, block_e, n_valid, xs_sorted, w_gu, b_gu, w_down, b_down)
        xs = _combine(run_n, tile_pos, slot_pos, xs, mod_l, gates, pos, y_sorted)
    return xs[:, ctx_len:, :]
```

```python
import functools

import numpy as np
import jax
import jax.numpy as jnp
from jax import lax
from jax.experimental import pallas as pl
from jax.experimental.pallas import tpu as pltpu

F32 = jnp.float32
BF16 = jnp.bfloat16
I32 = jnp.int32
HIGHEST = lax.Precision.HIGHEST

D_MODEL = 1024
NORM_EPS = 1e-6
CHUNK = 64
TILE = 256
CHUNKS_PER_TILE = TILE // CHUNK
SCAN_PAIR = 4
GRID_W = 64
ROPE_THETA = 10000.0

N_HEADS = 4
HEAD_DIM = 64
MIX_W = N_HEADS * HEAD_DIM
AT_HEADS = 4
AT_KV_HEADS = 2
AT_HEAD_DIM = 128
AT_GROUP = AT_HEADS // AT_KV_HEADS
ATT_KEY_CHUNK = 2176

HG_IN = 5 * MIX_W
ML_IN = 4 * MIX_W
AT_IN = (AT_HEADS + 2 * AT_KV_HEADS) * AT_HEAD_DIM
GATE_W = 4 * MIX_W

N_EXPERTS = 32
TOP_K = 4
D_EXPERT = 1024
SWIGLU_LIMIT = 7.0
SWIGLU_ALPHA = 1.702
MOE_BLOCK = 512
ROW_SUB = D_MODEL // 128
LANES = 128
N_MOD = 6 * D_MODEL
MOD_ROWS = 16

VMEM_LIMIT = 56 * 1024 * 1024


def _cparams(n_axes, vmem=None):
    kw = dict(dimension_semantics=("arbitrary",) * n_axes)
    if vmem is not None:
        kw["vmem_limit_bytes"] = vmem
    return pltpu.CompilerParams(**kw)


def _dot(a, b, precision=None):
    return jnp.dot(a, b, preferred_element_type=F32, precision=precision)


def _dot_nt(a, b):
    return lax.dot_general(a, b, (((1,), (1,)), ((), ())), preferred_element_type=F32)


def _dot_tn(a, b, precision=None):
    return lax.dot_general(a, b, (((0,), (0,)), ((), ())), preferred_element_type=F32,
                           precision=precision)


def _log_sigmoid(z):
    return jnp.minimum(z, 0.0) - jnp.log1p(jnp.exp(-jnp.abs(z)))


def _rms(x, eps=NORM_EPS):
    return x * lax.rsqrt(jnp.mean(x * x, axis=-1, keepdims=True) + eps)


def _mod_kernel(c_ref, w_ref, b_ref, o_ref):
    cv = c_ref[...]
    s = cv * (1.0 / (1.0 + jnp.exp(-cv)))
    o_ref[0] = _dot(s, w_ref[0], precision=HIGHEST) + b_ref[0]


def _modulation(cvec, w_mod, b_mod):
    depth = w_mod.shape[0]
    nblk = 1536
    return pl.pallas_call(
        _mod_kernel,
        grid=(depth, N_MOD // nblk),
        in_specs=[pl.BlockSpec((MOD_ROWS, D_MODEL), lambda l, n: (0, 0)),
                  pl.BlockSpec((1, D_MODEL, nblk), lambda l, n: (l, 0, n)),
                  pl.BlockSpec((1, 1, nblk), lambda l, n: (l, 0, n))],
        out_specs=pl.BlockSpec((1, MOD_ROWS, nblk), lambda l, n: (l, 0, n)),
        out_shape=jax.ShapeDtypeStruct((depth, MOD_ROWS, N_MOD), F32),
        compiler_params=_cparams(2, VMEM_LIMIT),
        name="modulation",
    )(cvec, w_mod, b_mod.reshape(depth, 1, N_MOD))


def _mod_spec(batch):
    return pl.BlockSpec((1, 1, N_MOD), lambda b, j: (jnp.where(j == 0, batch, b), 0, 0))


def _inproj_kernel(x_ref, mod_ref, g_ref, w_ref, cos_ref, sin_ref, qg_ref, kg_ref,
                   hg_ref, ml_ref, gt_ref, q_ref, k_ref, v_ref):
    x = x_ref[0]
    mod = mod_ref[0]
    sh = mod[:, 0:D_MODEL]
    sc = mod[:, D_MODEL:2 * D_MODEL]
    h = _rms(x) * g_ref[...]
    h = (h * (1.0 + sc) + sh).astype(BF16)
    o0 = 0
    for ref, width in ((hg_ref, HG_IN), (ml_ref, ML_IN), (gt_ref, GATE_W)):
        ref[0] = _dot(h, w_ref[:, o0:o0 + width])
        o0 += width

    at = _dot(h, w_ref[:, o0:o0 + AT_IN])
    cos = cos_ref[...]
    sin = sin_ref[...]
    lane = lax.broadcasted_iota(I32, (TILE, AT_HEAD_DIM), 1)
    quarter = AT_HEAD_DIM // 4
    first_half = (lane % (2 * quarter)) < quarter

    def norm_rope(y, g):
        y = _rms(y) * g
        partner = jnp.where(first_half, pltpu.roll(y, AT_HEAD_DIM - quarter, axis=1), pltpu.roll(y, quarter, axis=1))
        return y * cos + partner * sin

    head = lambda i: at[:, i * AT_HEAD_DIM:(i + 1) * AT_HEAD_DIM]
    for i in range(AT_HEADS):
        q_ref[0, :, i * AT_HEAD_DIM:(i + 1) * AT_HEAD_DIM] = (
            norm_rope(head(i), qg_ref[...]) * (AT_HEAD_DIM ** -0.5)).astype(BF16)
    for i in range(AT_KV_HEADS):
        k_ref[0, :, i * AT_HEAD_DIM:(i + 1) * AT_HEAD_DIM] = norm_rope(head(AT_HEADS + i), kg_ref[...]).astype(BF16)
        v_ref[0, :, 2 * i * AT_HEAD_DIM:(2 * i + 1) * AT_HEAD_DIM] = head(AT_HEADS + AT_KV_HEADS + i).astype(BF16)
        v_ref[0, :, (2 * i + 1) * AT_HEAD_DIM:(2 * i + 2) * AT_HEAD_DIM] = jnp.ones((TILE, AT_HEAD_DIM), BF16)


def _inproj(xs, mod_l, g1, w_in_r, cos, sin, q_g, k_g):
    batch, rows, _ = xs.shape
    nt = rows // TILE
    n_all = HG_IN + ML_IN + GATE_W + AT_IN
    tok = lambda w: pl.BlockSpec((1, TILE, w), lambda b, j: (b, j, 0))
    tab = pl.BlockSpec((TILE, AT_HEAD_DIM), lambda b, j: (j, 0))
    vec = pl.BlockSpec((1, AT_HEAD_DIM), lambda b, j: (0, 0))
    kvw = AT_KV_HEADS * AT_HEAD_DIM
    widths = ((HG_IN, F32), (ML_IN, F32), (GATE_W, F32), (AT_HEADS * AT_HEAD_DIM, BF16), (kvw, BF16), (2 * kvw, BF16))
    return pl.pallas_call(
        _inproj_kernel,
        grid=(batch, nt),
        in_specs=[tok(D_MODEL), _mod_spec(batch),
                  pl.BlockSpec((1, D_MODEL), lambda b, j: (0, 0)),
                  pl.BlockSpec((D_MODEL, n_all), lambda b, j: (0, 0)),
                  tab, tab, vec, vec],
        out_specs=[tok(w) for w, _ in widths],
        out_shape=[jax.ShapeDtypeStruct((batch, rows, w), dt) for w, dt in widths],
        compiler_params=_cparams(2, VMEM_LIMIT),
        name="inproj",
    )(xs, mod_l, g1, w_in_r, cos, sin, q_g, k_g)


def _tile_of(d, j, nt):
    return j + d * jnp.where(j > 0, nt - 2 * j, 0)


def _chunk_of(d, i):
    return i + d * (CHUNKS_PER_TILE - 1 - 2 * i)


_HG_LEVELS = (32, 16, 8, 4, 2, 1)


def _head_block_mask():
    h = np.arange(MIX_W) // HEAD_DIM
    return (h[:, None] == h[None, :]).astype(np.float32)


def _split_hi_lo(x):
    hi = x.astype(BF16)
    lo = (x - hi.astype(F32)).astype(BF16)
    return jnp.concatenate([hi, lo], axis=0)


def _block_diag_rows(x, lane_head):
    zero = jnp.zeros_like(x)
    return jnp.concatenate([jnp.where(lane_head == h, x, zero) for h in range(N_HEADS)], axis=0)


@functools.lru_cache(maxsize=None)
def _hgrn2_consts():
    c = CHUNK
    tri = np.tril(np.ones((c, c)))
    ops = [tri]
    masks = []
    pos = np.arange(c)
    for m in _HG_LEVELS:
        blk = pos // (2 * m)
        sel = np.zeros((c, c))
        sel[pos, 2 * m * blk + m - 1] = 1.0
        ops.append(tri - sel @ tri)
        in_b = (pos % (2 * m)) >= m
        masks.append(((blk[:, None] == blk[None, :]) & in_b[:, None] & ~in_b[None, :]).astype(np.float64))
    masks.append(np.eye(c))
    ops.append(np.ones((c, c)) - tri)
    flip = np.eye(c)[::-1]
    stack = lambda xs: np.concatenate(xs, 0)
    ops_d = np.stack([stack(ops), stack([flip @ o @ flip for o in ops])])
    ops_d = np.concatenate([ops_d, ops_d], axis=2)
    tile = lambda mk: np.tile(mk, (1, N_HEADS))
    masks_d = np.stack([np.stack([tile(mk) for mk in masks]), np.stack([tile(mk[::-1, ::-1]) for mk in masks])])
    return ops_d.astype(np.float32), masks_d.astype(np.float32)


def _hgrn2_kernel(q_ref, z_ref, v_ref, lb_ref, ops_ref, msk_ref, hm_ref, o_ref, st_scr, lf_scr, k_scr, *,
                  direction):
    d = direction
    j = pl.program_id(1)

    @pl.when(j == 0)
    def _():
        st_scr[...] = jnp.zeros_like(st_scr)

    lb = lb_ref[0]
    z = z_ref[...]
    a = jnp.log(lb)
    cterm = jnp.log1p(-lb) + _log_sigmoid(z)
    lf_scr[...] = jnp.maximum(a, cterm) + jnp.log1p(jnp.exp(-jnp.abs(a - cterm)))
    k_scr[...] = (1.0 - lb) / (1.0 + jnp.exp(z))
    n_lv = len(_HG_LEVELS)
    lane_head = lax.broadcasted_iota(I32, (CHUNK, MIX_W), 1) // HEAD_DIM

    for i, bb in [(i, bb) for i in range(CHUNKS_PER_TILE) for bb in range(SCAN_PAIR)]:
        rows = pl.ds(_chunk_of(d, i) * CHUNK, CHUNK)
        cs = _dot(ops_ref[0], _split_hi_lo(lf_scr[bb, rows, :]))
        b = cs[0:CHUNK]
        rest = cs[(n_lv + 1) * CHUNK:(n_lv + 2) * CHUNK]
        q = q_ref[bb, rows, :]
        k = k_scr[bb, rows, :]
        vb = v_ref[bb, rows, :].astype(BF16)
        q_in = (q * jnp.exp(b)).astype(BF16)
        k_out = (k * jnp.exp(rest)).astype(BF16)
        s_decay = jnp.exp(b[0:1] + rest[0:1])
        sc = jnp.zeros((CHUNK, MIX_W), F32)
        for lv in range(n_lv + 1):
            if lv < n_lv:
                e = jnp.exp(-jnp.abs(cs[(lv + 1) * CHUNK:(lv + 2) * CHUNK]))
                ql = (q * e).astype(BF16)
                kl = (k * e).astype(BF16)
            else:
                ql = q.astype(BF16)
                kl = k.astype(BF16)
            sc = sc + msk_ref[0, lv] * _dot_nt(ql, _block_diag_rows(kl, lane_head))
        st = st_scr[bb]
        o_ref[bb, rows, :] = (_dot(sc.astype(BF16), _block_diag_rows(vb, lane_head))
                              + _dot_nt(q_in, st.astype(BF16)))
        st_scr[bb] = st * s_decay + hm_ref[...] * _dot_tn(vb, k_out)


def _hgrn2(p_hg, lb_l):
    batch, rows, _ = p_hg.shape
    nt = rows // TILE
    ops_d, masks_d = _hgrn2_consts()
    n_ops = ops_d.shape[1]
    n_msk = masks_d.shape[1]
    def one_direction(d):
        col = lambda c: pl.BlockSpec((SCAN_PAIR, TILE, MIX_W), lambda b, j: (b, _tile_of(d, j, nt), c))
        const = lambda shp: pl.BlockSpec((1,) + shp, lambda b, j: (d,) + (0,) * len(shp))
        return pl.pallas_call(
            functools.partial(_hgrn2_kernel, direction=d),
            grid=(batch // SCAN_PAIR, nt),
            in_specs=[col(0), col(1 + d), col(3),
                      const((1, MIX_W)), const((n_ops, 2 * CHUNK)), const((n_msk, CHUNK, MIX_W)),
                      pl.BlockSpec((MIX_W, MIX_W), lambda b, j: (0, 0))],
            out_specs=col(0),
            out_shape=jax.ShapeDtypeStruct((batch, rows, MIX_W), F32),
            scratch_shapes=[pltpu.VMEM((SCAN_PAIR, MIX_W, MIX_W), F32),
                            pltpu.VMEM((SCAN_PAIR, TILE, MIX_W), F32),
                            pltpu.VMEM((SCAN_PAIR, TILE, MIX_W), F32)],
            compiler_params=_cparams(2, VMEM_LIMIT),
            name="hgrn2",
        )(p_hg, p_hg, p_hg, lb_l.reshape(2, 1, MIX_W), jnp.asarray(ops_d, dtype=BF16), jnp.asarray(masks_d),
          jnp.asarray(_head_block_mask()))

    return one_direction(0), one_direction(1)


@functools.lru_cache(maxsize=None)
def _mlstm_consts():
    c = CHUNK
    tri = np.tril(np.ones((c, c)))
    flip = np.eye(c)[::-1]
    tile = lambda mk: np.tile(mk, (1, N_HEADS))
    ops, masks = [], []
    for t in (tri, flip @ tri @ flip):
        ops.append(np.concatenate([t, t], axis=1))
        masks.append(np.stack([tile(t.T), tile(t), tile(np.eye(c))]))
    return np.stack(ops).astype(np.float32), np.stack(masks).astype(np.float32)


def _mlstm_kernel(q_ref, k_ref, v_ref, gi_ref, gf_ref, bi_ref, bf_ref, op_ref, cm_ref, hm_ref, h_ref,
                  ct_scr, n_scr, m_scr, i_scr, f_scr, *, direction):
    d = direction
    j = pl.program_id(1)

    @pl.when(j == 0)
    def _():
        ct_scr[...] = jnp.zeros_like(ct_scr)
        n_scr[...] = jnp.zeros_like(n_scr)
        m_scr[...] = jnp.zeros_like(m_scr)

    i_scr[...] = gi_ref[...] + bi_ref[0]
    f_scr[...] = _log_sigmoid(gf_ref[...] + bf_ref[0])
    lane_head = lax.broadcasted_iota(I32, (CHUNK, MIX_W), 1) // HEAD_DIM
    neg_inf = F32(-jnp.inf)
    hm = hm_ref[...]
    hm_b = hm.astype(BF16)

    for i, bb in [(i, bb) for i in range(CHUNKS_PER_TILE) for bb in range(SCAN_PAIR)]:
        rows = pl.ds(_chunk_of(d, i) * CHUNK, CHUNK)
        ie = i_scr[bb, rows, :]
        fe = f_scr[bb, rows, :]
        bc = _dot(op_ref[0], _split_hi_lo(fe))
        row = jnp.sum(ie * cm_ref[0, 2] - fe * cm_ref[0, 0], axis=0, keepdims=True)
        tot = jnp.sum(fe, axis=0, keepdims=True)
        m_st = m_scr[bb]
        dm = jnp.where(cm_ref[0, 1] > 0.5, bc + row, neg_inf)
        m_in = jnp.zeros((CHUNK, MIX_W), F32)
        for h in range(N_HEADS):
            mh = jnp.max(dm[:, h * HEAD_DIM:(h + 1) * HEAD_DIM], axis=-1, keepdims=True)
            m_in = jnp.where(lane_head == h, mh, m_in)
        from_prev = bc + m_st
        m_t = jnp.maximum(from_prev, m_in)
        q = q_ref[bb, rows, :]
        k = k_ref[bb, rows, :] * (HEAD_DIM ** -0.5)
        qb = q.astype(BF16)
        vb = v_ref[bb, rows, :].astype(BF16)
        w = jnp.exp(dm - m_t) * _dot_nt(qb, _block_diag_rows(k.astype(BF16), lane_head))
        a_prev = jnp.exp(from_prev - m_t)
        ct = ct_scr[bb]
        n_st = n_scr[bb]
        num = _dot(w.astype(BF16), _block_diag_rows(vb, lane_head)) + a_prev * _dot_nt(qb, ct.astype(BF16))
        den = _dot((w + a_prev * (q * n_st)).astype(BF16), hm_b)
        h_ref[bb, rows, :] = num / jnp.maximum(jnp.abs(den), jnp.exp(-m_t))
        e_col = tot - bc + ie
        m_new = jnp.maximum(tot + m_st, jnp.max(e_col, axis=0, keepdims=True))
        a_end = jnp.exp(tot + m_st - m_new)
        kw = k * jnp.exp(e_col - m_new)
        ct_scr[bb] = ct * a_end + hm * _dot_tn(vb, kw.astype(BF16))
        n_scr[bb] = a_end * n_st + jnp.sum(kw, axis=0, keepdims=True)
        m_scr[bb] = m_new


def _mlstm(p_ml, p_gt, bias_i, bias_f):
    batch, rows, _ = p_ml.shape
    nt = rows // TILE
    ops_d, masks_d = _mlstm_consts()
    def one_direction(d):
        col = lambda c: pl.BlockSpec((SCAN_PAIR, TILE, MIX_W), lambda b, j: (b, _tile_of(d, j, nt), c))
        const = lambda shp: pl.BlockSpec((1,) + shp, lambda b, j: (d,) + (0,) * len(shp))
        return pl.pallas_call(
            functools.partial(_mlstm_kernel, direction=d),
            grid=(batch // SCAN_PAIR, nt),
            in_specs=[col(0), col(1), col(2), col(2 * d), col(2 * d + 1),
                      const((1, MIX_W)), const((1, MIX_W)),
                      const((CHUNK, 2 * CHUNK)), const((3, CHUNK, MIX_W)),
                      pl.BlockSpec((MIX_W, MIX_W), lambda b, j: (0, 0))],
            out_specs=col(0),
            out_shape=jax.ShapeDtypeStruct((batch, rows, MIX_W), F32),
            scratch_shapes=[pltpu.VMEM((SCAN_PAIR, MIX_W, MIX_W), F32),
                            pltpu.VMEM((SCAN_PAIR, 1, MIX_W), F32),
                            pltpu.VMEM((SCAN_PAIR, 1, MIX_W), F32),
                            pltpu.VMEM((SCAN_PAIR, TILE, MIX_W), F32),
                            pltpu.VMEM((SCAN_PAIR, TILE, MIX_W), F32)],
            compiler_params=_cparams(2, VMEM_LIMIT),
            name="mlstm",
        )(p_ml, p_ml, p_ml, p_gt, p_gt, bias_i, bias_f, jnp.asarray(ops_d, dtype=BF16), jnp.asarray(masks_d),
          jnp.asarray(_head_block_mask()))

    return one_direction(0), one_direction(1)


def _rope_tables(rows, ctx_len):
    n = rows - ctx_len
    r = jnp.arange(n) // GRID_W
    cc = jnp.arange(n) % GRID_W
    axis_dims = AT_HEAD_DIM // 2
    inv_freq = ROPE_THETA ** (-jnp.arange(0, axis_dims, 2, dtype=F32) / axis_dims)
    ang_r = r.astype(F32)[:, None] * inv_freq
    ang_c = cc.astype(F32)[:, None] * inv_freq
    cos = jnp.concatenate([jnp.cos(ang_r), jnp.cos(ang_r), jnp.cos(ang_c), jnp.cos(ang_c)], -1)
    sin = jnp.concatenate([-jnp.sin(ang_r), jnp.sin(ang_r), -jnp.sin(ang_c), jnp.sin(ang_c)], -1)
    cos = jnp.concatenate([jnp.ones((ctx_len, AT_HEAD_DIM), F32), cos], 0)
    sin = jnp.concatenate([jnp.zeros((ctx_len, AT_HEAD_DIM), F32), sin], 0)
    return cos, sin


def _attn_kernel(q_ref, k_ref, v_ref, o_ref):
    j = pl.program_id(2)

    def attend(n_keys):
        q2 = jnp.concatenate([q_ref[0, :, g * AT_HEAD_DIM:(g + 1) * AT_HEAD_DIM] for g in range(AT_GROUP)], axis=0)
        m = jnp.full((AT_GROUP * TILE, 1), -jnp.inf, F32)
        acc = jnp.zeros((AT_GROUP * TILE, 2 * AT_HEAD_DIM), F32)
        bounds = list(range(0, n_keys, ATT_KEY_CHUNK))
        bounds = bounds[:max(len(bounds) - 1, 1)] if n_keys % ATT_KEY_CHUNK else bounds
        bounds.append(n_keys)
        for c0, c1 in zip(bounds[:-1], bounds[1:]):
            s = _dot_nt(q2, k_ref[0, c0:c1, :])
            m_new = jnp.maximum(m, jnp.max(s, axis=-1, keepdims=True))
            p = jnp.exp((s - m_new).astype(BF16))
            acc = acc * jnp.exp(m - m_new) + _dot(p, v_ref[0, c0:c1, :])
            m = m_new
        o = acc[:, 0:AT_HEAD_DIM] / acc[:, AT_HEAD_DIM:AT_HEAD_DIM + 1]
        for g in range(AT_GROUP):
            o_ref[0, :, g * AT_HEAD_DIM:(g + 1) * AT_HEAD_DIM] = o[g * TILE:(g + 1) * TILE]

    @pl.when(j == 0)
    def _():
        attend(TILE)

    @pl.when(j > 0)
    def _():
        attend(k_ref.shape[1])


def _attention(qn, kn, vb):
    batch, rows, _ = qn.shape
    nt = rows // TILE
    gw = AT_GROUP * AT_HEAD_DIM
    return pl.pallas_call(
        _attn_kernel,
        grid=(batch, AT_KV_HEADS, nt),
        in_specs=[pl.BlockSpec((1, TILE, gw), lambda b, h, j: (b, j, h)),
                  pl.BlockSpec((1, rows, AT_HEAD_DIM), lambda b, h, j: (b, 0, h)),
                  pl.BlockSpec((1, rows, 2 * AT_HEAD_DIM), lambda b, h, j: (b, 0, h))],
        out_specs=pl.BlockSpec((1, TILE, gw), lambda b, h, j: (b, j, h)),
        out_shape=jax.ShapeDtypeStruct((batch, rows, AT_HEADS * AT_HEAD_DIM), F32),
        compiler_params=_cparams(3, VMEM_LIMIT),
        name="attention",
    )(qn, kn, vb)


def _outproj_kernel(x_ref, mod_ref, ohf_ref, ohb_ref, gh_ref, omf_ref, omb_ref, gm_ref, at_ref,
                    hgg_ref, mlg_ref, ones_ref, wo_ref, g2_ref, wr_ref, br_ref, low_ref, up_ref,
                    xo_ref, st_ref, pos_ref, gate_ref, tab_ref, cnt_ref, cnt_scr):
    b = pl.program_id(0)
    j = pl.program_id(1)

    @pl.when((b == 0) & (j == 0))
    def _():
        cnt_scr[...] = jnp.zeros_like(cnt_scr)

    mod = mod_ref[0]
    gate1 = mod[:, 2 * D_MODEL:3 * D_MODEL]
    sh2 = mod[:, 3 * D_MODEL:4 * D_MODEL]
    sc2 = mod[:, 4 * D_MODEL:5 * D_MODEL]

    def head_norm(o, g):
        ms = _dot((o * o).astype(BF16), ones_ref[...]) * (1.0 / HEAD_DIM)
        return o * lax.rsqrt(ms + NORM_EPS) * g

    gh = gh_ref[0]
    y_hg = head_norm(ohf_ref[0] + ohb_ref[0], hgg_ref[...]) * (gh / (1.0 + jnp.exp(-gh)))
    y_ml = head_norm(omf_ref[0] + omb_ref[0], mlg_ref[...]) / (1.0 + jnp.exp(-gm_ref[0]))
    proj = (_dot(y_hg.astype(BF16), wo_ref[0:MIX_W, :])
            + _dot(y_ml.astype(BF16), wo_ref[MIX_W:2 * MIX_W, :])
            + _dot(at_ref[0].astype(BF16), wo_ref[2 * MIX_W:, :]))
    x = x_ref[0] + gate1 * proj
    xo_ref[0] = x
    h2 = _rms(x) * g2_ref[...]
    h2 = h2 * (1.0 + sc2) + sh2

    lane = lax.broadcasted_iota(I32, (TILE, LANES), 1)
    lane_f = lane.astype(F32)
    h_hi = h2.astype(BF16)
    h_lo = (h2 - h_hi.astype(F32)).astype(BF16)
    logits = _dot(jnp.concatenate([h_hi, h_lo, h_hi], axis=1), wr_ref[...]) + br_ref[...]
    lg = jnp.where(lane < N_EXPERTS, logits, -jnp.inf)
    multi = jnp.zeros((TILE, LANES), F32)
    hots, vals = [], []
    for _ in range(TOP_K):
        mx = jnp.max(lg, axis=-1, keepdims=True)
        idx = jnp.min(jnp.where(lg == mx, lane_f, F32(LANES)), axis=-1, keepdims=True)
        hot = lane_f == idx
        lg = jnp.where(hot, -jnp.inf, lg)
        multi = multi + hot.astype(F32)
        hots.append(hot)
        vals.append(mx)
    es = [jnp.exp(v - vals[0]) for v in vals]
    den = es[0] + es[1] + es[2] + es[3]

    tile_cnt = jnp.sum(multi, axis=0, keepdims=True)
    tile_start = _dot(jnp.broadcast_to(tile_cnt, (8, LANES)).astype(BF16), up_ref[...])[0:1]
    where_in_tile = _dot(low_ref[...], multi.astype(BF16)) + tile_start
    slot_lane = lax.broadcasted_iota(I32, (TILE, TOP_K * TILE), 1).astype(F32)
    onehot = jnp.zeros((TILE, TOP_K * TILE), F32)
    pos_out = jnp.zeros((TILE, LANES), I32)
    gate_out = jnp.zeros((TILE, LANES), F32)
    for kk in range(TOP_K):
        pos = jnp.sum(jnp.where(hots[kk], where_in_tile, 0.0), axis=-1, keepdims=True)
        onehot = onehot + jnp.where(slot_lane == pos, 1.0, 0.0)
        pos_out = pos_out + jnp.where(lane == kk, pos.astype(I32), 0)
        gate_out = gate_out + jnp.where(lane == kk, es[kk] / den, 0.0)
    sorted_rows = _dot_tn(onehot.astype(BF16), h_hi)
    st_ref[...] = sorted_rows.astype(BF16).reshape(TOP_K * TILE, ROW_SUB, LANES)
    pos_ref[0] = pos_out[:, 0:2 * TOP_K]
    gate_ref[0] = gate_out
    tab_ref[0] = jnp.concatenate([tile_cnt, tile_start, cnt_scr[...], jnp.zeros((5, LANES), F32)], axis=0).astype(I32)
    cnt_scr[...] = cnt_scr[...] + tile_cnt
    cnt_ref[...] = cnt_scr[...]


def _outproj(xs, mod_l, o_hg, p_hg, o_ml, p_ml, att, hg_g, ml_g, w_out_b, g2, w_router_p, b_router_p):
    batch, rows, _ = xs.shape
    nt = rows // TILE
    tok = lambda w, c=0: pl.BlockSpec((1, TILE, w), lambda b, j: (b, j, c))
    dirs = lambda d: tok(MIX_W)
    full = lambda r, c: pl.BlockSpec((r, c), lambda b, j: (0, 0))
    block_ones = np.kron(np.eye(N_HEADS), np.ones((HEAD_DIM, HEAD_DIM))).astype(np.float32)
    strict_low = np.tril(np.ones((TILE, TILE)), -1).astype(np.float32)
    strict_up = np.triu(np.ones((LANES, LANES)), 1).astype(np.float32)
    n_tiles = batch * nt
    return pl.pallas_call(
        _outproj_kernel,
        grid=(batch, nt),
        in_specs=[tok(D_MODEL), _mod_spec(batch),
                  dirs(0), dirs(1), tok(MIX_W, 4),
                  dirs(0), dirs(1), tok(MIX_W, 3),
                  tok(AT_HEADS * AT_HEAD_DIM),
                  full(1, MIX_W), full(1, MIX_W), full(MIX_W, MIX_W),
                  full(D_MODEL, D_MODEL), full(1, D_MODEL),
                  full(3 * D_MODEL, LANES), full(1, LANES), full(TILE, TILE), full(LANES, LANES)],
        out_specs=[tok(D_MODEL),
                   pl.BlockSpec((TOP_K * TILE, ROW_SUB, LANES), lambda b, j: (b * nt + j, 0, 0)),
                   tok(2 * TOP_K), tok(LANES),
                   pl.BlockSpec((1, 8, LANES), lambda b, j: (b * nt + j, 0, 0)),
                   pl.BlockSpec((1, LANES), lambda b, j: (0, 0))],
        out_shape=[jax.ShapeDtypeStruct((batch, rows, D_MODEL), F32),
                   jax.ShapeDtypeStruct((n_tiles * TOP_K * TILE, ROW_SUB, LANES), BF16),
                   jax.ShapeDtypeStruct((batch, rows, 2 * TOP_K), I32),
                   jax.ShapeDtypeStruct((batch, rows, LANES), F32),
                   jax.ShapeDtypeStruct((n_tiles, 8, LANES), I32),
                   jax.ShapeDtypeStruct((1, LANES), F32)],
        scratch_shapes=[pltpu.VMEM((1, LANES), F32)],
        compiler_params=_cparams(2, VMEM_LIMIT),
        name="outproj_router",
    )(xs, mod_l, o_hg[0], o_hg[1], p_hg, o_ml[0], o_ml[1], p_ml, att, hg_g, ml_g,
      jnp.asarray(block_ones, dtype=BF16),
      w_out_b, g2, w_router_p, b_router_p, jnp.asarray(strict_low, dtype=BF16),
      jnp.asarray(strict_up, dtype=BF16))


_RUN_BITS = tuple(1 << s for s in range(TILE.bit_length() - 1, -1, -1))


def _copy_run(src, dst, sem, n, src0, dst0, wait):
    for k, bit in enumerate(_RUN_BITS):
        done = n - (n & (2 * bit - 1))

        @pl.when((n & bit) != 0)
        def _():
            cp = pltpu.make_async_copy(src.at[pl.ds(src0 + done, bit)], dst.at[pl.ds(dst0 + done, bit)], sem)
            cp.wait() if wait else cp.start(priority=k % 2)


def _dispatch_kernel(n_ref, a_ref, b_ref, zn_ref, zo_ref, src_ref, dst_ref, zero_scr, sem):
    t = pl.program_id(0)

    @pl.when(t == 0)
    def _():
        zero_scr[...] = jnp.zeros_like(zero_scr)

    for wait in (False, True):
        def per_expert(e, carry):
            i = t * N_EXPERTS + e
            _copy_run(src_ref, dst_ref, sem, n_ref[i], a_ref[i], b_ref[i], wait)
            return carry

        lax.fori_loop(0, N_EXPERTS, per_expert, 0)

        @pl.when(t == 0)
        def _():
            def per_gap(g, carry):
                def per_piece(i, c):
                    left = jnp.minimum(zn_ref[g] - i * TILE, TILE)
                    _copy_run(zero_scr, dst_ref, sem, left, 0, zo_ref[g] + i * TILE, wait)
                    return c

                lax.fori_loop(0, (zn_ref[g] + TILE - 1) // TILE, per_piece, 0)
                return carry

            lax.fori_loop(0, zn_ref.shape[0], per_gap, 0)


def _dispatch(run_n, run_a, run_b, fill, h_tiles, n_slots):
    n_tiles = run_n.shape[0] // N_EXPERTS
    grid_spec = pltpu.PrefetchScalarGridSpec(
        num_scalar_prefetch=5,
        grid=(n_tiles,),
        in_specs=[pl.BlockSpec((TOP_K * TILE, ROW_SUB, LANES), lambda t, *_: (t, 0, 0))],
        out_specs=pl.BlockSpec(memory_space=pl.ANY),
        scratch_shapes=[pltpu.VMEM((TILE, ROW_SUB, LANES), h_tiles.dtype), pltpu.SemaphoreType.DMA(())],
    )
    return pl.pallas_call(
        _dispatch_kernel,
        grid_spec=grid_spec,
        out_shape=jax.ShapeDtypeStruct((n_slots, ROW_SUB, LANES), h_tiles.dtype),
        compiler_params=_cparams(1, VMEM_LIMIT),
        name="moe_dispatch",
    )(run_n, run_a, run_b, fill[0], fill[1], h_tiles)


def _expert_kernel(be_ref, nv_ref, x_ref, wgu_ref, bgu_ref, wd_ref, bd_ref, y_ref, wgu_scr, wd_scr):
    i = pl.program_id(0)

    @pl.when(i < nv_ref[0])
    def _():
        prev = be_ref[jnp.maximum(i - 1, 0)]

        @pl.when((i == 0) | (prev != be_ref[i]))
        def _():
            wgu_scr[...] = wgu_ref[0, 0].astype(BF16)
            wd_scr[...] = wd_ref[0, 0].astype(BF16)

        x = x_ref[...].reshape(MOE_BLOCK, D_MODEL)
        gu = _dot(x, wgu_scr[...]) + bgu_ref[0, 0]
        gate = jnp.minimum(gu[:, 0:D_EXPERT], SWIGLU_LIMIT)
        up = jnp.clip(gu[:, D_EXPERT:], -SWIGLU_LIMIT, SWIGLU_LIMIT)
        act = (up + 1.0) * gate * (1.0 / (1.0 + jnp.exp(-SWIGLU_ALPHA * gate)))
        y = _dot(act.astype(BF16), wd_scr[...]) + bd_ref[0, 0]
        y_ref[...] = y.astype(BF16).reshape(MOE_BLOCK, ROW_SUB, LANES)

    @pl.when(i >= nv_ref[0])
    def _():
        y_ref[...] = jnp.zeros_like(y_ref)


def _experts(layer, block_e, n_valid, xs_sorted, w_gu, b_gu, w_down, b_down):
    n_slots = xs_sorted.shape[0]
    nb = n_slots // MOE_BLOCK
    depth = w_gu.shape[0]
    clamp = lambda i, nv: jnp.minimum(i, jnp.maximum(nv[0] - 1, 0))
    grid_spec = pltpu.PrefetchScalarGridSpec(
        num_scalar_prefetch=2,
        grid=(nb,),
        in_specs=[pl.BlockSpec((MOE_BLOCK, ROW_SUB, LANES), lambda i, be, nv: (clamp(i, nv), 0, 0)),
                  pl.BlockSpec((1, 1, D_MODEL, 2 * D_EXPERT), lambda i, be, nv: (layer, be[i], 0, 0)),
                  pl.BlockSpec((1, 1, 1, 2 * D_EXPERT), lambda i, be, nv: (layer, be[i], 0, 0)),
                  pl.BlockSpec((1, 1, D_EXPERT, D_MODEL), lambda i, be, nv: (layer, be[i], 0, 0)),
                  pl.BlockSpec((1, 1, 1, D_MODEL), lambda i, be, nv: (layer, be[i], 0, 0))],
        out_specs=pl.BlockSpec((MOE_BLOCK, ROW_SUB, LANES), lambda i, be, nv: (i, 0, 0)),
        scratch_shapes=[pltpu.VMEM((D_MODEL, 2 * D_EXPERT), BF16),
                        pltpu.VMEM((D_EXPERT, D_MODEL), BF16)],
    )
    return pl.pallas_call(
        _expert_kernel,
        grid_spec=grid_spec,
        out_shape=jax.ShapeDtypeStruct((n_slots, ROW_SUB, LANES), BF16),
        compiler_params=_cparams(1, VMEM_LIMIT),
        name="moe_experts",
    )(block_e, n_valid, xs_sorted, w_gu, b_gu.reshape(depth, N_EXPERTS, 1, 2 * D_EXPERT),
      w_down, b_down.reshape(depth, N_EXPERTS, 1, D_MODEL))


def _combine_kernel(n_ref, a_ref, b_ref, x_ref, mod_ref, gate_ref, pos_ref, y_ref, o_ref, buf, sem):
    t = pl.program_id(0) * pl.num_programs(1) + pl.program_id(1)
    n_tiles = pl.num_programs(0) * pl.num_programs(1)
    cur = t % 2

    def runs(tile, which, wait):
        def per_expert(e, carry):
            i = tile * N_EXPERTS + e
            _copy_run(y_ref, buf.at[which], sem.at[which], n_ref[i], b_ref[i], a_ref[i], wait)
            return carry

        lax.fori_loop(0, N_EXPERTS, per_expert, 0)

    @pl.when(t == 0)
    def _():
        runs(t, cur, False)

    @pl.when(t + 1 < n_tiles)
    def _():
        runs(t + 1, 1 - cur, False)

    runs(t, cur, True)

    gates = gate_ref[0]
    pos = pos_ref[0].astype(F32)
    slot_lane = lax.broadcasted_iota(I32, (TILE, TOP_K * TILE), 1).astype(F32)
    wmat = jnp.zeros((TILE, TOP_K * TILE), F32)
    for kk in range(TOP_K):
        wmat = wmat + jnp.where(slot_lane == pos[:, kk:kk + 1], gates[:, kk:kk + 1], 0.0)
    f = _dot(wmat.astype(BF16), buf[cur].reshape(TOP_K * TILE, D_MODEL))
    gate2 = mod_ref[0][:, 5 * D_MODEL:6 * D_MODEL]
    o_ref[0] = x_ref[0] + gate2 * f


def _combine(run_n, run_a, run_b, xs, mod_l, gates, pos, y_sorted):
    batch, rows, _ = xs.shape
    nt = rows // TILE
    tok = lambda w: pl.BlockSpec((1, TILE, w), lambda b, j, *_: (b, j, 0))
    mod_spec = _mod_spec(batch)
    grid_spec = pltpu.PrefetchScalarGridSpec(
        num_scalar_prefetch=3,
        grid=(batch, nt),
        in_specs=[tok(D_MODEL),
                  pl.BlockSpec(mod_spec.block_shape, lambda b, j, *_: mod_spec.index_map(b, j)),
                  tok(LANES), tok(2 * TOP_K), pl.BlockSpec(memory_space=pl.ANY)],
        out_specs=tok(D_MODEL),
        scratch_shapes=[pltpu.VMEM((2, TOP_K * TILE, ROW_SUB, LANES), y_sorted.dtype),
                        pltpu.SemaphoreType.DMA((2,))],
    )
    return pl.pallas_call(
        _combine_kernel,
        grid_spec=grid_spec,
        out_shape=jax.ShapeDtypeStruct((batch, rows, D_MODEL), F32),
        compiler_params=_cparams(2, VMEM_LIMIT),
        name="moe_combine",
    )(run_n, run_a, run_b, xs, mod_l, gates, pos, y_sorted)


def _moe_plan(counts, tables, n_tokens):
    cnt = counts[0, :N_EXPERTS].astype(I32)
    padded = (cnt + MOE_BLOCK - 1) // MOE_BLOCK * MOE_BLOCK
    pend = jnp.cumsum(padded)
    pstart = pend - padded
    nb = -(-(n_tokens * TOP_K) // MOE_BLOCK) + N_EXPERTS
    block_row = jnp.arange(nb, dtype=I32) * MOE_BLOCK
    block_e = jnp.minimum(jnp.sum((pend[None, :] <= block_row[:, None]).astype(I32), axis=1), N_EXPERTS - 1)
    n_valid = (pend[-1] // MOE_BLOCK).astype(I32).reshape(1)
    run_n = tables[:, 0, :N_EXPERTS]
    tile_pos = tables[:, 1, :N_EXPERTS]
    slot_pos = tables[:, 2, :N_EXPERTS] + pstart[None, :]
    fill_n = jnp.concatenate([padded - cnt, (nb * MOE_BLOCK - pend[-1])[None]])
    fill_at = jnp.concatenate([pstart + cnt, pend[-1:]])
    flat = lambda a: a.reshape(-1).astype(I32)
    return (flat(run_n), flat(tile_pos), flat(slot_pos), (flat(fill_n), flat(fill_at)),
            block_e.astype(I32), n_valid, nb * MOE_BLOCK)


def _arrange_w_in(w_in_l):
    hg = w_in_l[:, 0:HG_IN]
    ml0 = HG_IN
    ml = w_in_l[:, ml0:ml0 + ML_IN]
    gt = w_in_l[:, ml0 + ML_IN:ml0 + ML_IN + 4 * N_HEADS]
    at = w_in_l[:, ml0 + ML_IN + 4 * N_HEADS:]
    gt = jnp.repeat(_gate_order(gt.T).T, HEAD_DIM, axis=1)
    return jnp.concatenate([hg, ml, gt, at], 1).astype(BF16)


def _gate_order(g):
    g = g.reshape((4, N_HEADS) + g.shape[1:]) if g.shape[0] == 4 * N_HEADS else g
    return jnp.concatenate([g[0], g[2], g[1], g[3]], axis=0)


def _hgrn2_lower_bounds(lb_logits):
    p = jax.nn.softmax(lb_logits.astype(F32), axis=1)
    cum = jnp.cumsum(p, axis=1)
    return cum - cum[:, :1]


def kernel(x, c, ctx, c_ctx, w_mod, b_mod, norm1_g, norm2_g, w_in, w_out, hg_lb_logits, hg_norm_g,
           ml_gate_bias, ml_norm_g, q_norm_g, k_norm_g, w_router, b_router, w_gu, b_gu, w_down, b_down):
    batch, seq, d = x.shape
    ctx_len = ctx.shape[1]
    depth = w_mod.shape[0]
    assert d == D_MODEL and ctx_len == TILE and seq % TILE == 0 and batch + 1 <= MOD_ROWS
    assert batch % SCAN_PAIR == 0
    rows = ctx_len + seq
    n_tokens = batch * rows

    cvec = jnp.concatenate([c, c_ctx[None, :], jnp.zeros((MOD_ROWS - batch - 1, d), F32)], 0)
    mod = _modulation(cvec, w_mod, b_mod).reshape(depth, MOD_ROWS, 1, N_MOD)
    hg_lb = _hgrn2_lower_bounds(hg_lb_logits)
    cos, sin = _rope_tables(rows, ctx_len)
    xs = jnp.concatenate([ctx, x], axis=1)

    for layer in range(depth):
        mod_l = mod[layer]
        p_hg, p_ml, p_gt, qn, kn, vb = _inproj(xs, mod_l, norm1_g[layer][None, :], _arrange_w_in(w_in[layer]),
                                               cos, sin, q_norm_g[layer][None, :], k_norm_g[layer][None, :])
        o_hg = _hgrn2(p_hg, hg_lb[:, layer])
        gb = jnp.repeat(_gate_order(ml_gate_bias[layer]), HEAD_DIM).reshape(4, 1, MIX_W)
        o_ml = _mlstm(p_ml, p_gt, gb[0::2], gb[1::2])
        att = _attention(qn, kn, vb)
        w_router_p = jnp.pad(w_router[layer], ((0, 0), (0, LANES - N_EXPERTS)))
        wr_hi = w_router_p.astype(BF16)
        wr_lo = (w_router_p - wr_hi.astype(F32)).astype(BF16)
        w_router_p = jnp.concatenate([wr_hi, wr_hi, wr_lo], axis=0)
        b_router_p = jnp.pad(b_router[layer], (0, LANES - N_EXPERTS))[None, :]
        xs, h_tiles, pos, gates, tables, counts = _outproj(
            xs, mod_l, o_hg, p_hg, o_ml, p_ml, att,
            jnp.tile(hg_norm_g[layer], N_HEADS)[None, :], jnp.tile(ml_norm_g[layer], N_HEADS)[None, :],
            w_out[layer].astype(BF16), norm2_g[layer][None, :], w_router_p, b_router_p)
        run_n, tile_pos, slot_pos, fill, block_e, n_valid, n_slots = _moe_plan(counts, tables, n_tokens)
        xs_sorted = _dispatch(run_n, tile_pos, slot_pos, fill, h_tiles, n_slots)
        y_sorted = _experts(layer, block_e, n_valid, xs_sorted, w_gu, b_gu, w_down, b_down)
        xs = _combine(run_n, tile_pos, slot_pos, xs, mod_l, gates, pos, y_sorted)
    return xs[:, ctx_len:, :]
```
